```python
import math
import jax, jax.numpy as jnp
from jax import lax
import numpy as np

D_MODEL = 2048
BATCH = 4
SEQ = 2048
DEPTH = 4
DEC_BATCH = 8
DEC_SEQ = 1
PAST_LEN = 16384
PAGE_SIZE = 128

N_PAGES = PAST_LEN // PAGE_SIZE
N_POOL = (DEC_BATCH * N_PAGES * 5) // 4

D_MIX = D_MODEL
D_DA = D_MIX // 2
D_RET = D_MIX - D_DA
DA_HEAD_DIM = 128
DA_HEADS = D_DA // (2 * DA_HEAD_DIM)
RET_HEADS = 4
RET_DV = D_RET // RET_HEADS
RET_DK = RET_DV // 2
RET_CHUNK = 128
Q_BLOCK = 128
ROPE_BASE = 10000.0
EPS = 1e-6
D_IN = 4 * D_DA + 2 * RET_HEADS * RET_DK + 2 * D_RET

kernel_name = 'hymba_diffattn_retnet_decode_step'


def _rms(x, g):
    xf = x.astype(jnp.float32)
    y = xf * lax.rsqrt(jnp.mean(xf * xf, axis=-1, keepdims=True) + EPS)
    if g is not None:
        y = y * g.astype(jnp.float32)
    return y.astype(x.dtype)


def _rotary(x, pos):
    half = x.shape[-1] // 2
    theta = 1.0 / (ROPE_BASE ** jnp.linspace(0.0, 1.0, half, dtype=jnp.float32))
    ang = pos.astype(jnp.float32)[:, None] * theta[None, :]
    cos = jnp.cos(ang)[:, None, :]
    sin = jnp.sin(ang)[:, None, :]
    xf = x.astype(jnp.float32)
    x1, x2 = xf[..., :half], xf[..., half:]
    return jnp.concatenate([x1 * cos - x2 * sin, x1 * sin + x2 * cos], axis=-1).astype(x.dtype)


def _log_decay():
    return jnp.log(1.0 - 2.0 ** (-5.0 - jnp.arange(RET_HEADS, dtype=jnp.float32)))


def _project(x, norm_g, w_in, qn_g, kn_g, pos):
    B, L, _ = x.shape
    z = _rms(x, norm_g) @ w_in
    sizes = [D_DA, D_DA, D_DA, D_DA, RET_HEADS * RET_DK, RET_HEADS * RET_DK, D_RET, D_RET]
    cuts = np.cumsum(sizes)[:-1].tolist()
    q, k, v, g_da, rq, rk, rv, g_ret = jnp.split(z, cuts, axis=-1)
    q = _rms(q.reshape(B, L, DA_HEADS, 2, DA_HEAD_DIM), qn_g)
    k = _rms(k.reshape(B, L, DA_HEADS, 2, DA_HEAD_DIM), kn_g)
    v = v.reshape(B, L, DA_HEADS, 2 * DA_HEAD_DIM)
    rq = _rotary(rq.reshape(B, L, RET_HEADS, RET_DK), pos)
    rk = _rotary(rk.reshape(B, L, RET_HEADS, RET_DK) * (RET_DK ** -0.5), pos)
    rv = rv.reshape(B, L, RET_HEADS, RET_DV)
    return q, k, v, g_da, rq, rk, rv, g_ret


def _merge(att, ret, g_da, g_ret, subln_g, lam_init, w_out):
    B, L = att.shape[:2]
    a = (_rms(att, subln_g) * (1.0 - lam_init)).reshape(B, L, D_DA) * jax.nn.silu(g_da)
    r = _rms(ret, None).reshape(B, L, D_RET) * jax.nn.silu(g_ret)
    return jnp.concatenate([a, r], axis=-1) @ w_out


def _diff_combine(p, lam):
    return p[:, :, 0] - lam * p[:, :, 1]


def _diff_attn_prompt(q, k, v, lam):
    B, S = q.shape[:2]
    nb = S // Q_BLOCK
    scale = DA_HEAD_DIM ** -0.5
    qb = q.reshape(B, nb, Q_BLOCK, DA_HEADS, 2, DA_HEAD_DIM).swapaxes(0, 1)
    kpos = jnp.arange(S)

    def block(args):
        qi, i = args
        s = jnp.einsum('bqhcd,bkhcd->bhcqk', qi, k).astype(jnp.float32) * scale
        qpos = i * Q_BLOCK + jnp.arange(Q_BLOCK)
        mask = kpos[None, :] <= qpos[:, None]
        s = jnp.where(mask, s, -jnp.inf)
        w = _diff_combine(jax.nn.softmax(s, axis=-1), lam)
        return jnp.einsum('bhqk,bkhv->bqhv', w.astype(v.dtype), v)

    out = lax.map(block, (qb, jnp.arange(nb)))
    return out.swapaxes(0, 1).reshape(B, S, DA_HEADS, 2 * DA_HEAD_DIM)


def _diff_attn_decode(q, k_new, v_new, k_past, v_past, lam):
    T = q.shape[1]
    P = k_past.shape[1]
    scale = DA_HEAD_DIM ** -0.5
    s_past = jnp.einsum('bqhcd,bkhcd->bhcqk', q, k_past).astype(jnp.float32) * scale
    s_new = jnp.einsum('bqhcd,bkhcd->bhcqk', q, k_new).astype(jnp.float32) * scale
    tri = jnp.tril(jnp.ones((T, T), dtype=bool))
    s_new = jnp.where(tri, s_new, -jnp.inf)
    p = jax.nn.softmax(jnp.concatenate([s_past, s_new], axis=-1), axis=-1)
    w = _diff_combine(p, lam).astype(v_new.dtype)
    return (jnp.einsum('bhqk,bkhv->bqhv', w[..., :P], v_past)
            + jnp.einsum('bhqk,bkhv->bqhv', w[..., P:], v_new))


def _retention_chunk(state, q, k, v, log_g):
    L = q.shape[1]
    qf, kf, vf = q.astype(jnp.float32), k.astype(jnp.float32), v.astype(jnp.float32)
    idx = jnp.arange(L, dtype=jnp.float32)
    diff = idx[:, None] - idx[None, :]
    dmask = jnp.where(diff[None] >= 0, jnp.exp(log_g[:, None, None] * jnp.maximum(diff, 0.0)[None]), 0.0)
    s = jnp.einsum('blhk,bmhk->bhlm', qf, kf) * dmask[None]
    intra = jnp.einsum('bhlm,bmhv->blhv', s, vf)
    cross_dec = jnp.exp((idx[:, None] + 1.0) * log_g[None, :])
    cross = jnp.einsum('blhk,bhkv->blhv', qf, state) * cross_dec[None, :, :, None]
    k_dec = kf * jnp.exp((L - 1.0 - idx)[:, None] * log_g[None, :])[None, :, :, None]
    new_state = state * jnp.exp(L * log_g)[None, :, None, None] + jnp.einsum('blhk,blhv->bhkv', k_dec, vf)
    return new_state, intra + cross


def _retention_prompt(q, k, v, log_g):
    B, L, H, _ = q.shape
    nc = L // RET_CHUNK

    def to_chunks(t):
        return t.reshape(B, nc, RET_CHUNK, *t.shape[2:]).swapaxes(0, 1)

    def step(s, inp):
        qc, kc, vc = inp
        return _retention_chunk(s, qc, kc, vc, log_g)

    state0 = jnp.zeros((B, H, RET_DK, RET_DV), jnp.float32)
    state, out = lax.scan(step, state0, (to_chunks(q), to_chunks(k), to_chunks(v)))
    out = out.swapaxes(0, 1).reshape(B, L, H, RET_DV)
    return out.astype(q.dtype), state


def setup_inputs(seed: int = 0) -> dict:
    key = jax.random.key(seed)
    ks = jax.random.split(key, 16)
    f32 = jnp.float32
    x_prompt = jax.random.normal(ks[0], (BATCH, SEQ, D_MODEL), f32)
    x_sample = jax.random.normal(ks[1], (DEC_BATCH, DEC_SEQ, D_MODEL), f32)
    cache_k = jax.random.normal(ks[2], (DEPTH, N_POOL, PAGE_SIZE, DA_HEADS, 2, DA_HEAD_DIM), f32)
    cache_v = jax.random.normal(ks[3], (DEPTH, N_POOL, PAGE_SIZE, DA_HEADS, 2 * DA_HEAD_DIM), f32)
    state_ret = jax.random.normal(ks[4], (DEPTH, DEC_BATCH, RET_HEADS, RET_DK, RET_DV), f32) * 0.1
    perm = jax.random.permutation(ks[5], N_POOL)
    page_table = perm[:DEC_BATCH * N_PAGES].reshape(DEC_BATCH, N_PAGES).astype(jnp.int32)
    norm_g = 1.0 + 0.02 * jax.random.normal(ks[6], (DEPTH, D_MODEL), f32)
    w_in = jax.random.normal(ks[7], (DEPTH, D_MODEL, D_IN), f32) * D_MODEL ** -0.5
    w_out = jax.random.normal(ks[8], (DEPTH, D_MIX, D_MODEL), f32) * D_MIX ** -0.5
    qn_g = 1.0 + 0.02 * jax.random.normal(ks[9], (DEPTH, DA_HEAD_DIM), f32)
    kn_g = 1.0 + 0.02 * jax.random.normal(ks[10], (DEPTH, DA_HEAD_DIM), f32)
    lam_q1 = 0.1 * jax.random.normal(ks[11], (DEPTH, DA_HEAD_DIM), f32)
    lam_k1 = 0.1 * jax.random.normal(ks[12], (DEPTH, DA_HEAD_DIM), f32)
    lam_q2 = 0.1 * jax.random.normal(ks[13], (DEPTH, DA_HEAD_DIM), f32)
    lam_k2 = 0.1 * jax.random.normal(ks[14], (DEPTH, DA_HEAD_DIM), f32)
    subln_g = 1.0 + 0.02 * jax.random.normal(ks[15], (DEPTH, 2 * DA_HEAD_DIM), f32)
    return {'x_prompt': x_prompt, 'x_sample': x_sample, 'cache_k': cache_k, 'cache_v': cache_v,
            'state_ret': state_ret, 'page_table': page_table, 'norm_g': norm_g, 'w_in': w_in,
            'w_out': w_out, 'qn_g': qn_g, 'kn_g': kn_g, 'lam_q1': lam_q1, 'lam_k1': lam_k1,
            'lam_q2': lam_q2, 'lam_k2': lam_k2, 'subln_g': subln_g}


def reference(x_prompt, x_sample, cache_k, cache_v, state_ret, page_table, norm_g, w_in, w_out,
              qn_g, kn_g, lam_q1, lam_k1, lam_q2, lam_k2, subln_g):
    f32 = jnp.float32
    log_g = _log_decay()
    pos_p = jnp.arange(SEQ)
    pos_s = PAST_LEN + jnp.arange(DEC_SEQ)
    n_dec, n_pages = page_table.shape
    xp, xs = x_prompt, x_sample
    kp_l, vp_l, sp_l, ks_l, vs_l, ss_l = [], [], [], [], [], []
    for l in range(DEPTH):
        lam_init = 0.8 - 0.6 * math.exp(-0.3 * l)
        lam = (jnp.exp(jnp.sum(lam_q1[l].astype(f32) * lam_k1[l].astype(f32)))
               - jnp.exp(jnp.sum(lam_q2[l].astype(f32) * lam_k2[l].astype(f32))) + lam_init)

        q, k, v, g_da, rq, rk, rv, g_ret = _project(xp, norm_g[l], w_in[l], qn_g[l], kn_g[l], pos_p)
        att = _diff_attn_prompt(q, k, v, lam)
        ret, st = _retention_prompt(rq, rk, rv, log_g)
        xp = xp + _merge(att, ret, g_da, g_ret, subln_g[l], lam_init, w_out[l])
        kp_l.append(k)
        vp_l.append(v)
        sp_l.append(st)

        q, k, v, g_da, rq, rk, rv, g_ret = _project(xs, norm_g[l], w_in[l], qn_g[l], kn_g[l], pos_s)
        k_past = cache_k[l][page_table].reshape(n_dec, n_pages * PAGE_SIZE, DA_HEADS, 2, DA_HEAD_DIM)
        v_past = cache_v[l][page_table].reshape(n_dec, n_pages * PAGE_SIZE, DA_HEADS, 2 * DA_HEAD_DIM)
        att = _diff_attn_decode(q, k, v, k_past, v_past, lam)
        st, ret = _retention_chunk(state_ret[l].astype(f32), rq, rk, rv, log_g)
        xs = xs + _merge(att, ret.astype(xs.dtype), g_da, g_ret, subln_g[l], lam_init, w_out[l])
        ks_l.append(k)
        vs_l.append(v)
        ss_l.append(st)

    return (xp, xs, jnp.stack(kp_l), jnp.stack(vp_l), jnp.stack(sp_l),
            jnp.stack(ks_l), jnp.stack(vs_l), jnp.stack(ss_l))
```

```python
import functools
import math

import numpy as np
import jax
import jax.numpy as jnp
from jax import lax
from jax.experimental import pallas as pl
from jax.experimental.pallas import tpu as pltpu

D_MODEL = 2048
BATCH = 4
SEQ = 2048
DEPTH = 4
DEC_BATCH = 8
DEC_SEQ = 1
PAST_LEN = 16384
PAGE_SIZE = 128
N_PAGES = PAST_LEN // PAGE_SIZE
N_POOL = (DEC_BATCH * N_PAGES * 5) // 4

D_DA = D_MODEL // 2
D_RET = D_MODEL - D_DA
DA_HEAD_DIM = 128
DA_HEADS = D_DA // (2 * DA_HEAD_DIM)
DA_DV = 2 * DA_HEAD_DIM
RET_HEADS = 4
RET_DV = D_RET // RET_HEADS
RET_DK = RET_DV // 2
ROPE_BASE = 10000.0
EPS = 1e-6
D_IN = 4 * D_DA + 2 * RET_HEADS * RET_DK + 2 * D_RET

COL_Q, COL_K, COL_V, COL_GDA = 0, D_DA, 2 * D_DA, 3 * D_DA
COL_RQ = 4 * D_DA
COL_RK = COL_RQ + RET_HEADS * RET_DK
COL_RV = COL_RK + RET_HEADS * RET_DK
COL_GRET = COL_RV + D_RET

LANES = 128
SAMPLE_ROWS = 16
VMEM_LIMIT = 48 * 1024 * 1024

PROMPT_TM = 512
MERGE_TM = 256
ATTN_TQ = 256
RET_CHUNK = 128
DEC_PAGES_PER_STEP = 8

LOG_DECAY = np.log(1.0 - 2.0 ** (-5.0 - np.arange(RET_HEADS, dtype=np.float32))).astype(np.float32)

F32 = jnp.float32
BF16 = jnp.bfloat16


def _cparams(n_axes):
    return pltpu.CompilerParams(dimension_semantics=("arbitrary",) * n_axes,
                                vmem_limit_bytes=VMEM_LIMIT)


def _silu(g):
    return g / (1.0 + jnp.exp(-g))


def _rms_rows(x):
    return x * lax.rsqrt(jnp.mean(x * x, axis=-1, keepdims=True) + EPS)


def _dot_nt(a, b):
    return lax.dot_general(a, b, (((1,), (1,)), ((), ())), preferred_element_type=F32)


def _dot_tn(a, b):
    return lax.dot_general(a, b, (((0,), (0,)), ((), ())), preferred_element_type=F32)


def _norm_kernel(x_ref, g_ref, h_ref):
    h_ref[...] = (_rms_rows(x_ref[...]) * g_ref[...]).astype(BF16)


def _input_norm(x, g, tm):
    t = x.shape[0]
    return pl.pallas_call(
        _norm_kernel,
        out_shape=jax.ShapeDtypeStruct((t, D_MODEL), BF16),
        grid=(t // tm,),
        in_specs=[pl.BlockSpec((tm, D_MODEL), lambda m: (m, 0)),
                  pl.BlockSpec((1, D_MODEL), lambda m: (0, 0))],
        out_specs=pl.BlockSpec((tm, D_MODEL), lambda m: (m, 0)),
        compiler_params=_cparams(1),
        name="input_norm",
    )(x, g)


def _proj_qk_kernel(h_ref, w_ref, g_ref, *out_refs, scale):
    z = jnp.dot(h_ref[...], w_ref[...], preferred_element_type=F32)
    g = g_ref[...]
    for j in range(D_DA // DA_HEAD_DIM):
        sl = slice(j * DA_HEAD_DIM, (j + 1) * DA_HEAD_DIM)
        y = _rms_rows(z[:, sl]) * g
        if scale is not None:
            y = y * scale
        for o in out_refs:
            o[:, sl] = y.astype(o.dtype)


def _proj_copy_kernel(h_ref, w_ref, *out_refs):
    z = jnp.dot(h_ref[...], w_ref[...], preferred_element_type=F32)
    for o in out_refs:
        o[...] = z.astype(o.dtype)


def _proj_rot_kernel(h_ref, w_ref, cos_ref, sin_ref, rq_ref, rk_ref):
    z = jnp.dot(h_ref[...], w_ref[...], preferred_element_type=F32)
    cos2 = cos_ref[...]
    sin2 = sin_ref[...]
    for j in range(2 * RET_HEADS):
        x = z[:, j * RET_DK:(j + 1) * RET_DK]
        if j >= RET_HEADS:
            x = x * (RET_DK ** -0.5)
        y = x * cos2 + pltpu.roll(x, RET_DK // 2, axis=1) * sin2
        if j < RET_HEADS:
            rq_ref[:, j * RET_DK:(j + 1) * RET_DK] = y.astype(rq_ref.dtype)
        else:
            jj = j - RET_HEADS
            rk_ref[:, jj * RET_DK:(jj + 1) * RET_DK] = y.astype(rk_ref.dtype)


def _proj_call(kern, h, w_in, layer, col0, ncols, tm, extra, extra_specs, outs, name):
    t = h.shape[0]
    assert col0 % ncols == 0
    cb = col0 // ncols
    out_shape = [jax.ShapeDtypeStruct((t, w), dt) for (w, dt) in outs]
    out_specs = [pl.BlockSpec((tm, w), lambda m: (m, 0)) for (w, _) in outs]
    return pl.pallas_call(
        kern,
        out_shape=out_shape,
        grid=(t // tm,),
        in_specs=[pl.BlockSpec((tm, D_MODEL), lambda m: (m, 0)),
                  pl.BlockSpec((None, D_MODEL, ncols), lambda m: (layer, 0, cb))] + extra_specs,
        out_specs=out_specs,
        compiler_params=_cparams(1),
        name=name,
    )(h, w_in, *extra)


def _project(h, w_in, layer, qn_g, kn_g, cos2, sin2, tm, act_dtype):
    t = h.shape[0]
    gspec = [pl.BlockSpec((1, DA_HEAD_DIM), lambda m: (0, 0))]
    (q,) = _proj_call(functools.partial(_proj_qk_kernel, scale=DA_HEAD_DIM ** -0.5),
                      h, w_in, layer, COL_Q, D_DA, tm, [qn_g], gspec, [(D_DA, act_dtype)], "proj_q")
    k32, kact = _proj_call(functools.partial(_proj_qk_kernel, scale=None),
                           h, w_in, layer, COL_K, D_DA, tm, [kn_g], gspec,
                           [(D_DA, F32), (D_DA, act_dtype)], "proj_k")
    v32, vact = _proj_call(_proj_copy_kernel, h, w_in, layer, COL_V, D_DA, tm, [], [],
                           [(D_DA, F32), (D_DA, act_dtype)], "proj_v")
    (g_da,) = _proj_call(_proj_copy_kernel, h, w_in, layer, COL_GDA, D_DA, tm, [], [],
                         [(D_DA, F32)], "proj_gda")
    n_tab = cos2.shape[0] // tm
    tspec = [pl.BlockSpec((tm, RET_DK), lambda m: (m % n_tab, 0))] * 2
    nrot = 2 * RET_HEADS * RET_DK
    rq, rk = _proj_call(_proj_rot_kernel, h, w_in, layer, COL_RQ, nrot, tm, [cos2, sin2], tspec,
                        [(nrot // 2, act_dtype), (nrot // 2, act_dtype)], "proj_rot")
    (rv,) = _proj_call(_proj_copy_kernel, h, w_in, layer, COL_RV, D_RET, tm, [], [],
                       [(D_RET, act_dtype)], "proj_rv")
    (g_ret,) = _proj_call(_proj_copy_kernel, h, w_in, layer, COL_GRET, D_RET, tm, [], [],
                          [(D_RET, F32)], "proj_gret")
    return q, k32, kact, v32, vact, g_da, rq, rk, rv, g_ret


def _lam_value(lq1, lk1, lq2, lk2, lam_init):
    s1 = jnp.sum(lq1 * lk1, axis=-1, keepdims=True)
    s2 = jnp.sum(lq2 * lk2, axis=-1, keepdims=True)
    return jnp.exp(s1) - jnp.exp(s2) + lam_init


def _subln_gate(o, subg, gate, lam_init):
    return (_rms_rows(o) * subg) * (1.0 - lam_init) * _silu(gate)


def _attn_kernel(lq1_ref, lk1_ref, lq2_ref, lk2_ref, subg_ref, q_ref, k_ref, v_ref, g_ref, o_ref,
                 m_ref, l_ref, acc_ref, *, lam_init, tq):
    qi = pl.program_id(2)
    q = q_ref[...]
    d = DA_HEAD_DIM

    m_ref[...] = jnp.full(m_ref.shape, -jnp.inf, F32)
    l_ref[...] = jnp.zeros(l_ref.shape, F32)
    acc_ref[...] = jnp.zeros(acc_ref.shape, F32)

    def step(k, v, mask):
        for c in range(2):
            s = _dot_nt(q[:, c * d:(c + 1) * d], k[:, c * d:(c + 1) * d])
            if mask is not None:
                s = jnp.where(mask, s, -jnp.inf)
            m_prev = m_ref[c]
            m_new = jnp.maximum(m_prev, jnp.max(s, axis=-1, keepdims=True))
            alpha = jnp.exp(m_prev - m_new)
            p = jnp.exp(s - m_new)
            l_ref[c] = alpha * l_ref[c] + jnp.sum(p, axis=-1, keepdims=True)
            acc_ref[c] = alpha * acc_ref[c] + jnp.dot(p.astype(BF16), v, preferred_element_type=F32)
            m_ref[c] = m_new

    def body(ki, carry):
        off = pl.multiple_of(ki * tq, tq)
        step(k_ref[pl.ds(off, tq), :], v_ref[pl.ds(off, tq), :], None)
        return carry

    lax.fori_loop(0, qi, body, 0)

    off = pl.multiple_of(qi * tq, tq)
    row = lax.broadcasted_iota(jnp.int32, (tq, tq), 0)
    col = lax.broadcasted_iota(jnp.int32, (tq, tq), 1)
    step(k_ref[pl.ds(off, tq), :], v_ref[pl.ds(off, tq), :], col <= row)

    lam = _lam_value(lq1_ref[...], lk1_ref[...], lq2_ref[...], lk2_ref[...], lam_init)
    o = acc_ref[0] / l_ref[0] - lam * (acc_ref[1] / l_ref[1])
    o_ref[...] = _subln_gate(o, subg_ref[...], g_ref[...], lam_init).astype(o_ref.dtype)


def _prompt_attention(q, k, v, g_da, lam_vecs, subg, lam_init):
    tq = ATTN_TQ
    nq = SEQ // tq
    vec = pl.BlockSpec((1, DA_HEAD_DIM), lambda b, h, i: (0, 0))
    return pl.pallas_call(
        functools.partial(_attn_kernel, lam_init=lam_init, tq=tq),
        out_shape=jax.ShapeDtypeStruct((BATCH * SEQ, D_DA), BF16),
        grid=(BATCH, DA_HEADS, nq),
        in_specs=[vec, vec, vec, vec,
                  pl.BlockSpec((1, DA_DV), lambda b, h, i: (0, 0)),
                  pl.BlockSpec((tq, DA_DV), lambda b, h, i: (b * nq + i, h)),
                  pl.BlockSpec((SEQ, DA_DV), lambda b, h, i: (b, h)),
                  pl.BlockSpec((SEQ, DA_DV), lambda b, h, i: (b, h)),
                  pl.BlockSpec((tq, DA_DV), lambda b, h, i: (b * nq + i, h))],
        out_specs=pl.BlockSpec((tq, DA_DV), lambda b, h, i: (b * nq + i, h)),
        scratch_shapes=[pltpu.VMEM((2, tq, 1), F32), pltpu.VMEM((2, tq, 1), F32),
                        pltpu.VMEM((2, tq, DA_DV), F32)],
        compiler_params=_cparams(3),
        name="prompt_attention",
    )(*lam_vecs, subg, q, k, v, g_da)


def _ret_kernel(logg_ref, q_ref, k_ref, v_ref, g_ref, o_ref, st_ref, *, chunk):
    lg = logg_ref[pl.program_id(1)]
    c = chunk
    li = lax.broadcasted_iota(jnp.int32, (c, 1), 0).astype(F32)
    diff = (lax.broadcasted_iota(jnp.int32, (c, c), 0)
            - lax.broadcasted_iota(jnp.int32, (c, c), 1)).astype(F32)
    dmask = jnp.where(diff >= 0, jnp.exp(lg * jnp.maximum(diff, 0.0)), 0.0)
    cross_dec = jnp.exp((li + 1.0) * lg)
    k_dec = jnp.exp((c - 1.0 - li) * lg)
    st_dec = jnp.exp(jnp.full((1, 1), c, F32) * lg)

    def body(i, state):
        off = pl.multiple_of(i * c, c)
        q = q_ref[pl.ds(off, c), :]
        k = k_ref[pl.ds(off, c), :]
        v = v_ref[pl.ds(off, c), :]
        s = _dot_nt(q, k) * dmask
        intra = jnp.dot(s.astype(BF16), v, preferred_element_type=F32)
        cross = jnp.dot(q, state.astype(BF16), preferred_element_type=F32) * cross_dec
        kd = (k.astype(F32) * k_dec).astype(BF16)
        new_state = state * st_dec + _dot_tn(kd, v)
        ret = _rms_rows(intra + cross) * _silu(g_ref[pl.ds(off, c), :])
        o_ref[pl.ds(off, c), :] = ret.astype(o_ref.dtype)
        return new_state

    st_ref[...] = lax.fori_loop(0, SEQ // c, body, jnp.zeros((RET_DK, RET_DV), F32))


def _prompt_retention(rq, rk, rv, g_ret):
    return pl.pallas_call(
        functools.partial(_ret_kernel, chunk=RET_CHUNK),
        out_shape=[jax.ShapeDtypeStruct((BATCH * SEQ, D_RET), BF16),
                   jax.ShapeDtypeStruct((BATCH, RET_HEADS, RET_DK, RET_DV), F32)],
        grid=(BATCH, RET_HEADS),
        in_specs=[pl.BlockSpec(memory_space=pltpu.SMEM),
                  pl.BlockSpec((SEQ, RET_DK), lambda b, h: (b, h)),
                  pl.BlockSpec((SEQ, RET_DK), lambda b, h: (b, h)),
                  pl.BlockSpec((SEQ, RET_DV), lambda b, h: (b, h)),
                  pl.BlockSpec((SEQ, RET_DV), lambda b, h: (b, h))],
        out_specs=[pl.BlockSpec((SEQ, RET_DV), lambda b, h: (b, h)),
                   pl.BlockSpec((None, None, RET_DK, RET_DV), lambda b, h: (b, h, 0, 0))],
        compiler_params=_cparams(2),
        name="prompt_retention",
    )(jnp.asarray(LOG_DECAY), rq, rk, rv, g_ret)


def _merge_kernel(a_ref, r_ref, w_ref, x_ref, g_ref, y_ref, *h_ref):
    ar = jnp.concatenate([a_ref[...], r_ref[...]], axis=-1)
    y = x_ref[...] + jnp.dot(ar, w_ref[...], preferred_element_type=F32)
    y_ref[...] = y
    if h_ref:
        h_ref[0][...] = (_rms_rows(y) * g_ref[...]).astype(BF16)


def _merge(a, r, w_out, layer, x, g_next, tm):
    t = x.shape[0]
    with_h = g_next is not None
    out_shape = [jax.ShapeDtypeStruct((t, D_MODEL), F32)]
    out_specs = [pl.BlockSpec((tm, D_MODEL), lambda m: (m, 0))]
    if with_h:
        out_shape.append(jax.ShapeDtypeStruct((t, D_MODEL), BF16))
        out_specs.append(pl.BlockSpec((tm, D_MODEL), lambda m: (m, 0)))
    else:
        g_next = jnp.ones((1, D_MODEL), F32)
    res = pl.pallas_call(
        _merge_kernel,
        out_shape=out_shape,
        grid=(t // tm,),
        in_specs=[pl.BlockSpec((tm, D_DA), lambda m: (m, 0)),
                  pl.BlockSpec((tm, D_RET), lambda m: (m, 0)),
                  pl.BlockSpec((None, D_MODEL, D_MODEL), lambda m: (layer, 0, 0)),
                  pl.BlockSpec((tm, D_MODEL), lambda m: (m, 0)),
                  pl.BlockSpec((1, D_MODEL), lambda m: (0, 0))],
        out_specs=out_specs,
        compiler_params=_cparams(1),
        name="merge",
    )(a, r, w_out, x, g_next)
    return (res[0], res[1]) if with_h else (res[0], None)


def _dec_attn_kernel(pt_ref, lq1_ref, lk1_ref, lq2_ref, lk2_ref, subg_ref,
                     q_ref, kn_ref, vn_ref, g_ref, *rest, lam_init, npg):
    del pt_ref
    k_refs = rest[:npg]
    v_refs = rest[npg:2 * npg]
    o_ref = rest[2 * npg]
    qm_ref, m_ref, l_ref, acc_ref = rest[2 * npg + 1:]
    p_idx = pl.program_id(1)
    rows = SAMPLE_ROWS

    @pl.when(p_idx == 0)
    def _():
        r = lax.broadcasted_iota(jnp.int32, (rows, D_DA), 0)
        cgrp = lax.broadcasted_iota(jnp.int32, (rows, D_DA), 1) // DA_HEAD_DIM
        qb = jnp.broadcast_to(q_ref[...], (rows, D_DA))
        qm_ref[...] = jnp.where(r == cgrp, qb, 0.0).astype(BF16)
        m_ref[...] = jnp.full(m_ref.shape, -jnp.inf, F32)
        l_ref[...] = jnp.zeros(l_ref.shape, F32)
        acc_ref[...] = jnp.zeros(acc_ref.shape, F32)

    def update(k, v, ntok_valid):
        s = _dot_nt(qm_ref[...], k)
        if ntok_valid is not None:
            tcol = lax.broadcasted_iota(jnp.int32, s.shape, 1)
            s = jnp.where(tcol < ntok_valid, s, -jnp.inf)
        m_prev = m_ref[...]
        m_new = jnp.maximum(m_prev, jnp.max(s, axis=-1, keepdims=True))
        alpha = jnp.exp(m_prev - m_new)
        p = jnp.exp(s - m_new)
        l_ref[...] = alpha * l_ref[...] + jnp.sum(p, axis=-1, keepdims=True)
        acc_ref[...] = alpha * acc_ref[...] + jnp.dot(p.astype(BF16), v, preferred_element_type=F32)
        m_ref[...] = m_new

    for i in range(npg):
        update(k_refs[i][...].astype(BF16), v_refs[i][...].astype(BF16), None)

    @pl.when(p_idx == pl.num_programs(1) - 1)
    def _():
        r = lax.broadcasted_iota(jnp.int32, (rows, D_DA), 0)
        kn = jnp.where(r == 0, jnp.broadcast_to(kn_ref[...], (rows, D_DA)), 0.0).astype(BF16)
        vn = jnp.where(r == 0, jnp.broadcast_to(vn_ref[...], (rows, D_DA)), 0.0).astype(BF16)
        update(kn, vn, 1)
        lam = _lam_value(lq1_ref[...], lk1_ref[...], lq2_ref[...], lk2_ref[...], lam_init)
        o = acc_ref[...] / l_ref[...]
        g = g_ref[...]
        subg = subg_ref[...]
        for h in range(DA_HEADS):
            sl = slice(h * DA_DV, (h + 1) * DA_DV)
            att = o[2 * h:2 * h + 1, sl] - lam * o[2 * h + 1:2 * h + 2, sl]
            o_ref[:, sl] = _subln_gate(att, subg, g[:, sl], lam_init)


def _decode_attention(q, k_new, v_new, g_da, cache_k, cache_v, page_table, layer, lam_vecs, subg,
                      lam_init):
    npg = DEC_PAGES_PER_STEP
    vec = pl.BlockSpec((1, DA_HEAD_DIM), lambda b, p, pt: (0, 0))
    row = pl.BlockSpec((None, 1, D_DA), lambda b, p, pt: (b, 0, 0))

    def page_spec(i):
        return pl.BlockSpec((None, PAGE_SIZE, D_DA),
                            lambda b, p, pt: (layer * N_POOL + pt[b, p * npg + i], 0, 0))

    grid_spec = pltpu.PrefetchScalarGridSpec(
        num_scalar_prefetch=1,
        grid=(DEC_BATCH, N_PAGES // npg),
        in_specs=[vec, vec, vec, vec, pl.BlockSpec((1, DA_DV), lambda b, p, pt: (0, 0)),
                  row, row, row, row]
                 + [page_spec(i) for i in range(npg)] * 2,
        out_specs=pl.BlockSpec((None, 1, D_DA), lambda b, p, pt: (b, 0, 0)),
        scratch_shapes=[pltpu.VMEM((SAMPLE_ROWS, D_DA), BF16),
                        pltpu.VMEM((SAMPLE_ROWS, 1), F32), pltpu.VMEM((SAMPLE_ROWS, 1), F32),
                        pltpu.VMEM((SAMPLE_ROWS, D_DA), F32)],
    )
    return pl.pallas_call(
        functools.partial(_dec_attn_kernel, lam_init=lam_init, npg=npg),
        out_shape=jax.ShapeDtypeStruct((DEC_BATCH, 1, D_DA), F32),
        grid_spec=grid_spec,
        compiler_params=_cparams(2),
        name="decode_attention",
    )(page_table, *lam_vecs, subg, q, k_new, v_new, g_da,
      *([cache_k] * npg), *([cache_v] * npg))


def _dec_ret_kernel(st_ref, qc_ref, kc_ref, v_ref, g_ref, nst_ref, o_ref):
    v_all = v_ref[...]
    g_all = g_ref[...]
    for h in range(RET_HEADS):
        dec = float(1.0 - 2.0 ** (-5.0 - h))
        sl = slice(h * RET_DV, (h + 1) * RET_DV)
        st = st_ref[h]
        qc = qc_ref[h]
        kc = kc_ref[h]
        v = v_all[:, sl]
        nst_ref[h] = st * dec + kc * v
        cross = jnp.sum(qc * st, axis=0, keepdims=True) * dec
        intra = jnp.sum(qc * kc, axis=0, keepdims=True) * v
        o_ref[:, sl] = _rms_rows(intra + cross) * _silu(g_all[:, sl])


def _decode_retention(state, rq_col, rk_col, rv, g_ret, layer):
    col = pl.BlockSpec((None, RET_HEADS, RET_DK, 1), lambda b: (b, 0, 0, 0))
    row = pl.BlockSpec((None, 1, D_RET), lambda b: (b, 0, 0))
    return pl.pallas_call(
        _dec_ret_kernel,
        out_shape=[jax.ShapeDtypeStruct((DEC_BATCH, RET_HEADS, RET_DK, RET_DV), F32),
                   jax.ShapeDtypeStruct((DEC_BATCH, 1, D_RET), F32)],
        grid=(DEC_BATCH,),
        in_specs=[pl.BlockSpec((None, None, RET_HEADS, RET_DK, RET_DV), lambda b: (layer, b, 0, 0, 0)),
                  col, col, row, row],
        out_specs=[pl.BlockSpec((None, RET_HEADS, RET_DK, RET_DV), lambda b: (b, 0, 0, 0)),
                   pl.BlockSpec((None, 1, D_RET), lambda b: (b, 0, 0))],
        compiler_params=_cparams(1),
        name="decode_retention",
    )(state, rq_col, rk_col, rv, g_ret)


def _rotary_tables(pos):
    half = RET_DK // 2
    theta = 1.0 / (ROPE_BASE ** jnp.linspace(0.0, 1.0, half, dtype=F32))
    ang = pos.astype(F32)[:, None] * theta[None, :]
    cos, sin = jnp.cos(ang), jnp.sin(ang)
    return jnp.concatenate([cos, cos], axis=-1), jnp.concatenate([-sin, sin], axis=-1)


def _pad_rows(x):
    return jnp.pad(x, ((0, SAMPLE_ROWS - x.shape[0]), (0, 0)))


def kernel(x_prompt, x_sample, cache_k, cache_v, state_ret, page_table, norm_g, w_in, w_out,
           qn_g, kn_g, lam_q1, lam_k1, lam_q2, lam_k2, subln_g):
    t_p = BATCH * SEQ
    w_in_bf = w_in.astype(BF16)
    w_out_bf = w_out.astype(BF16)
    cos_p, sin_p = _rotary_tables(jnp.arange(SEQ))
    cos_s, sin_s = _rotary_tables(jnp.full((SAMPLE_ROWS,), PAST_LEN))
    ck = cache_k.reshape(DEPTH * N_POOL, PAGE_SIZE, D_DA)
    cv = cache_v.reshape(DEPTH * N_POOL, PAGE_SIZE, D_DA)

    xp = x_prompt.reshape(t_p, D_MODEL)
    xs = _pad_rows(x_sample.reshape(DEC_BATCH, D_MODEL))
    hp = _input_norm(xp, norm_g[0][None], PROMPT_TM)
    hs = _input_norm(xs, norm_g[0][None], SAMPLE_ROWS)

    kp_l, vp_l, sp_l, ks_l, vs_l, ss_l = [], [], [], [], [], []
    for l in range(DEPTH):
        lam_init = 0.8 - 0.6 * math.exp(-0.3 * l)
        lam_vecs = [a[l][None] for a in (lam_q1, lam_k1, lam_q2, lam_k2)]
        subg = subln_g[l][None]
        g_next = norm_g[l + 1][None] if l + 1 < DEPTH else None

        q, k32, kbf, v32, vbf, g_da, rq, rk, rv, g_ret = _project(
            hp, w_in_bf, l, qn_g[l][None], kn_g[l][None], cos_p, sin_p, PROMPT_TM, BF16)
        a = _prompt_attention(q, kbf, vbf, g_da, lam_vecs, subg, lam_init)
        r, st = _prompt_retention(rq, rk, rv, g_ret)
        xp, hp = _merge(a, r, w_out_bf, l, xp, g_next, MERGE_TM)
        kp_l.append(k32.reshape(BATCH, SEQ, DA_HEADS, 2, DA_HEAD_DIM))
        vp_l.append(v32.reshape(BATCH, SEQ, DA_HEADS, DA_DV))
        sp_l.append(st)

        q, k32, _, v32, _, g_da, rq, rk, rv, g_ret = _project(
            hs, w_in_bf, l, qn_g[l][None], kn_g[l][None], cos_s, sin_s, SAMPLE_ROWS, F32)
        as_row = lambda z: z[:DEC_BATCH].reshape(DEC_BATCH, 1, z.shape[-1])
        as_col = lambda z: z[:DEC_BATCH].reshape(DEC_BATCH, RET_HEADS, RET_DK, 1)
        a = _decode_attention(as_row(q), as_row(k32), as_row(v32), as_row(g_da), ck, cv, page_table,
                              l, lam_vecs, subg, lam_init)
        nst, r = _decode_retention(state_ret, as_col(rq), as_col(rk), as_row(rv), as_row(g_ret), l)
        a16 = _pad_rows(a.reshape(DEC_BATCH, D_DA)).astype(BF16)
        r16 = _pad_rows(r.reshape(DEC_BATCH, D_RET)).astype(BF16)
        xs, hs = _merge(a16, r16, w_out_bf, l, xs, g_next, SAMPLE_ROWS)
        ks_l.append(k32[:DEC_BATCH].reshape(DEC_BATCH, DEC_SEQ, DA_HEADS, 2, DA_HEAD_DIM))
        vs_l.append(v32[:DEC_BATCH].reshape(DEC_BATCH, DEC_SEQ, DA_HEADS, DA_DV))
        ss_l.append(nst)

    return (xp.reshape(BATCH, SEQ, D_MODEL),
            xs[:DEC_BATCH].reshape(DEC_BATCH, DEC_SEQ, D_MODEL),
            jnp.stack(kp_l), jnp.stack(vp_l), jnp.stack(sp_l),
            jnp.stack(ks_l), jnp.stack(vs_l), jnp.stack(ss_l))
```

```python
import functools
import math

import numpy as np
import jax
import jax.numpy as jnp
from jax import lax
from jax.experimental import pallas as pl
from jax.experimental.pallas import tpu as pltpu

D_MODEL = 2048
BATCH = 4
SEQ = 2048
DEPTH = 4
DEC_BATCH = 8
DEC_SEQ = 1
PAST_LEN = 16384
PAGE_SIZE = 128
N_PAGES = PAST_LEN // PAGE_SIZE
N_POOL = (DEC_BATCH * N_PAGES * 5) // 4

D_DA = D_MODEL // 2
D_RET = D_MODEL - D_DA
DA_HEAD_DIM = 128
DA_HEADS = D_DA // (2 * DA_HEAD_DIM)
DA_DV = 2 * DA_HEAD_DIM
RET_HEADS = 4
RET_DV = D_RET // RET_HEADS
RET_DK = RET_DV // 2
ROPE_BASE = 10000.0
EPS = 1e-6
D_IN = 4 * D_DA + 2 * RET_HEADS * RET_DK + 2 * D_RET

COL_Q, COL_K, COL_V, COL_GDA = 0, D_DA, 2 * D_DA, 3 * D_DA
COL_RQ = 4 * D_DA
COL_RK = COL_RQ + RET_HEADS * RET_DK
COL_RV = COL_RK + RET_HEADS * RET_DK
COL_GRET = COL_RV + D_RET

LANES = 128
SAMPLE_ROWS = 16
VMEM_LIMIT = 48 * 1024 * 1024

PROMPT_TM = 512
MERGE_TM = 256
ATTN_TQ = 512
Q_SCALE = DA_HEAD_DIM ** -0.5 * math.log2(math.e)
RET_CHUNK = 128
DEC_PAGES_PER_STEP = 8

LOG_DECAY = np.log(1.0 - 2.0 ** (-5.0 - np.arange(RET_HEADS, dtype=np.float32))).astype(np.float32)

F32 = jnp.float32
BF16 = jnp.bfloat16


def _cparams(n_axes):
    return pltpu.CompilerParams(dimension_semantics=("arbitrary",) * n_axes,
                                vmem_limit_bytes=VMEM_LIMIT)


def _silu(g):
    return g / (1.0 + jnp.exp(-g))


def _rms_rows(x):
    return x * lax.rsqrt(jnp.mean(x * x, axis=-1, keepdims=True) + EPS)


def _dot_nt(a, b):
    return lax.dot_general(a, b, (((1,), (1,)), ((), ())), preferred_element_type=F32)


def _dot_tn(a, b):
    return lax.dot_general(a, b, (((0,), (0,)), ((), ())), preferred_element_type=F32)


def _norm_kernel(x_ref, g_ref, h_ref):
    h_ref[...] = (_rms_rows(x_ref[...]) * g_ref[...]).astype(BF16)


def _input_norm(x, g, tm):
    t = x.shape[0]
    return pl.pallas_call(
        _norm_kernel,
        out_shape=jax.ShapeDtypeStruct((t, D_MODEL), BF16),
        grid=(t // tm,),
        in_specs=[pl.BlockSpec((tm, D_MODEL), lambda m: (m, 0)),
                  pl.BlockSpec((1, D_MODEL), lambda m: (0, 0))],
        out_specs=pl.BlockSpec((tm, D_MODEL), lambda m: (m, 0)),
        compiler_params=_cparams(1),
        name="input_norm",
    )(x, g)


def _proj_qk_kernel(h_ref, w_ref, g_ref, *out_refs, scale):
    z = jnp.dot(h_ref[...], w_ref[...], preferred_element_type=F32)
    tm = z.shape[0]
    g = g_ref[...]
    n_grp = D_DA // DA_HEAD_DIM
    for j in range(n_grp):
        sl = slice(j * DA_HEAD_DIM, (j + 1) * DA_HEAD_DIM)
        y = _rms_rows(z[:, sl]) * g
        if scale is not None:
            y = y * scale
        for o in out_refs:
            if o.shape[1] == DA_HEAD_DIM:
                o[pl.ds(j, tm, stride=n_grp), :] = y.astype(o.dtype)
            else:
                o[:, sl] = y.astype(o.dtype)


def _proj_v_kernel(h_ref, w_ref, v_ref, vact_ref):
    z = jnp.dot(h_ref[...], w_ref[...], preferred_element_type=F32)
    tm = z.shape[0]
    for h in range(DA_HEADS):
        for j in range(DA_DV // LANES):
            col = h * DA_DV + j * LANES
            v_ref[pl.ds(j * DA_HEADS + h, tm, stride=2 * DA_HEADS), :] = z[:, col:col + LANES]
    vact_ref[...] = z.astype(vact_ref.dtype)


def _proj_copy_kernel(h_ref, w_ref, *out_refs):
    z = jnp.dot(h_ref[...], w_ref[...], preferred_element_type=F32)
    for o in out_refs:
        o[...] = z.astype(o.dtype)


def _proj_rot_kernel(h_ref, w_ref, cos_ref, sin_ref, rq_ref, rk_ref):
    z = jnp.dot(h_ref[...], w_ref[...], preferred_element_type=F32)
    cos2 = cos_ref[...]
    sin2 = sin_ref[...]
    for j in range(2 * RET_HEADS):
        x = z[:, j * RET_DK:(j + 1) * RET_DK]
        if j >= RET_HEADS:
            x = x * (RET_DK ** -0.5)
        y = x * cos2 + pltpu.roll(x, RET_DK // 2, axis=1) * sin2
        if j < RET_HEADS:
            rq_ref[:, j * RET_DK:(j + 1) * RET_DK] = y.astype(rq_ref.dtype)
        else:
            jj = j - RET_HEADS
            rk_ref[:, jj * RET_DK:(jj + 1) * RET_DK] = y.astype(rk_ref.dtype)


def _proj_call(kern, h, w_in, layer, col0, ncols, tm, extra, extra_specs, outs, name):
    t = h.shape[0]
    assert col0 % ncols == 0
    cb = col0 // ncols
    out_shape = [jax.ShapeDtypeStruct((t * r, w), dt) for (r, w, dt) in outs]
    out_specs = [pl.BlockSpec((tm * r, w), lambda m: (m, 0)) for (r, w, _) in outs]
    return pl.pallas_call(
        kern,
        out_shape=out_shape,
        grid=(t // tm,),
        in_specs=[pl.BlockSpec((tm, D_MODEL), lambda m: (m, 0)),
                  pl.BlockSpec((None, D_MODEL, ncols), lambda m: (layer, 0, cb))] + extra_specs,
        out_specs=out_specs,
        compiler_params=_cparams(1),
        name=name,
    )(h, w_in, *extra)


def _project(h, w_in, layer, qn_g, kn_g, cos2, sin2, tm, act_dtype):
    gspec = [pl.BlockSpec((1, DA_HEAD_DIM), lambda m: (0, 0))]
    cache_rows = 2 * DA_HEADS
    (q,) = _proj_call(functools.partial(_proj_qk_kernel, scale=Q_SCALE),
                      h, w_in, layer, COL_Q, D_DA, tm, [qn_g], gspec, [(1, D_DA, act_dtype)], "proj_q")
    k32, kact = _proj_call(functools.partial(_proj_qk_kernel, scale=None),
                           h, w_in, layer, COL_K, D_DA, tm, [kn_g], gspec,
                           [(cache_rows, DA_HEAD_DIM, F32), (1, D_DA, act_dtype)], "proj_k")
    v32, vact = _proj_call(_proj_v_kernel, h, w_in, layer, COL_V, D_DA, tm, [], [],
                           [(cache_rows, LANES, F32), (1, D_DA, act_dtype)], "proj_v")
    (g_da,) = _proj_call(_proj_copy_kernel, h, w_in, layer, COL_GDA, D_DA, tm, [], [],
                         [(1, D_DA, F32)], "proj_gda")
    n_tab = cos2.shape[0] // tm
    tspec = [pl.BlockSpec((tm, RET_DK), lambda m: (m % n_tab, 0))] * 2
    nrot = 2 * RET_HEADS * RET_DK
    rq, rk = _proj_call(_proj_rot_kernel, h, w_in, layer, COL_RQ, nrot, tm, [cos2, sin2], tspec,
                        [(1, nrot // 2, act_dtype), (1, nrot // 2, act_dtype)], "proj_rot")
    (rv,) = _proj_call(_proj_copy_kernel, h, w_in, layer, COL_RV, D_RET, tm, [], [],
                       [(1, D_RET, act_dtype)], "proj_rv")
    (g_ret,) = _proj_call(_proj_copy_kernel, h, w_in, layer, COL_GRET, D_RET, tm, [], [],
                          [(1, D_RET, F32)], "proj_gret")
    return q, k32, kact, v32, vact, g_da, rq, rk, rv, g_ret


def _keys_from_cache_order(k32, lead):
    return k32.reshape(lead + (DA_HEADS, 2, DA_HEAD_DIM))


def _values_from_cache_order(v32, lead):
    n_half = DA_DV // LANES
    v = v32.reshape(lead + (n_half, DA_HEADS, LANES))
    nd = len(lead)
    v = v.transpose(tuple(range(nd)) + (nd + 1, nd, nd + 2))
    return v.reshape(lead + (DA_HEADS, DA_DV))


def _values_to_cache_order(cache_v):
    n_half = DA_DV // LANES
    v = cache_v.reshape(-1, PAGE_SIZE, DA_HEADS, n_half, LANES).transpose(0, 1, 3, 2, 4)
    return v.reshape(-1, PAGE_SIZE * n_half * DA_HEADS, LANES)


def _lam_value(lq1, lk1, lq2, lk2, lam_init):
    s1 = jnp.sum(lq1 * lk1, axis=-1, keepdims=True)
    s2 = jnp.sum(lq2 * lk2, axis=-1, keepdims=True)
    return jnp.exp(s1) - jnp.exp(s2) + lam_init


def _subln_gate(o, subg, gate, lam_init):
    return (_rms_rows(o) * subg) * (1.0 - lam_init) * _silu(gate)


def _attn_kernel(lq1_ref, lk1_ref, lq2_ref, lk2_ref, subg_ref, q_ref, k_ref, v_ref, g_ref, o_ref,
                 m_ref, l_ref, acc_ref, *, lam_init, tq):
    qi = pl.program_id(2)
    q = q_ref[...]
    d = DA_HEAD_DIM

    m_ref[...] = jnp.full(m_ref.shape, -jnp.inf, F32)
    l_ref[...] = jnp.zeros(l_ref.shape, F32)
    acc_ref[...] = jnp.zeros(acc_ref.shape, F32)

    def step(k, v, mask):
        upd = []
        for c in range(2):
            s = _dot_nt(q[:, c * d:(c + 1) * d], k[:, c * d:(c + 1) * d])
            if mask is not None:
                s = jnp.where(mask, s, -jnp.inf)
            m_prev = m_ref[c]
            m_new = jnp.maximum(m_prev, jnp.max(s, axis=-1, keepdims=True))
            alpha = jnp.exp2(m_prev - m_new)
            p = jnp.exp2(s - m_new)
            psum = p[:, :LANES]
            for j in range(1, tq // LANES):
                psum = psum + p[:, j * LANES:(j + 1) * LANES]
            pv = jnp.dot(p.astype(BF16), v, preferred_element_type=F32)
            upd.append((m_new, alpha, psum, pv))
        for c, (m_new, alpha, psum, pv) in enumerate(upd):
            m_ref[c] = m_new
            l_ref[c] = alpha * l_ref[c] + psum
            acc_ref[c] = alpha * acc_ref[c] + pv

    def body(ki, carry):
        off = pl.multiple_of(ki * tq, tq)
        step(k_ref[pl.ds(off, tq), :], v_ref[pl.ds(off, tq), :], None)
        return carry

    lax.fori_loop(0, qi, body, 0)

    off = pl.multiple_of(qi * tq, tq)
    row = lax.broadcasted_iota(jnp.int32, (tq, tq), 0)
    col = lax.broadcasted_iota(jnp.int32, (tq, tq), 1)
    step(k_ref[pl.ds(off, tq), :], v_ref[pl.ds(off, tq), :], col <= row)

    lam = _lam_value(lq1_ref[...], lk1_ref[...], lq2_ref[...], lk2_ref[...], lam_init)
    l1 = jnp.sum(l_ref[0], axis=-1, keepdims=True)
    l2 = jnp.sum(l_ref[1], axis=-1, keepdims=True)
    o = acc_ref[0] / l1 - lam * (acc_ref[1] / l2)
    o_ref[...] = _subln_gate(o, subg_ref[...], g_ref[...], lam_init).astype(o_ref.dtype)


def _prompt_attention(q, k, v, g_da, lam_vecs, subg, lam_init):
    tq = ATTN_TQ
    nq = SEQ // tq
    vec = pl.BlockSpec((1, DA_HEAD_DIM), lambda b, h, i: (0, 0))
    return pl.pallas_call(
        functools.partial(_attn_kernel, lam_init=lam_init, tq=tq),
        out_shape=jax.ShapeDtypeStruct((BATCH * SEQ, D_DA), BF16),
        grid=(BATCH, DA_HEADS, nq),
        in_specs=[vec, vec, vec, vec,
                  pl.BlockSpec((1, DA_DV), lambda b, h, i: (0, 0)),
                  pl.BlockSpec((tq, DA_DV), lambda b, h, i: (b * nq + i, h)),
                  pl.BlockSpec((SEQ, DA_DV), lambda b, h, i: (b, h)),
                  pl.BlockSpec((SEQ, DA_DV), lambda b, h, i: (b, h)),
                  pl.BlockSpec((tq, DA_DV), lambda b, h, i: (b * nq + i, h))],
        out_specs=pl.BlockSpec((tq, DA_DV), lambda b, h, i: (b * nq + i, h)),
        scratch_shapes=[pltpu.VMEM((2, tq, 1), F32), pltpu.VMEM((2, tq, LANES), F32),
                        pltpu.VMEM((2, tq, DA_DV), F32)],
        compiler_params=_cparams(3),
        name="prompt_attention",
    )(*lam_vecs, subg, q, k, v, g_da)


def _ret_kernel(logg_ref, q_ref, k_ref, v_ref, g_ref, o_ref, st_ref, *, chunk):
    lg = logg_ref[pl.program_id(1)]
    c = chunk
    li = lax.broadcasted_iota(jnp.int32, (c, 1), 0).astype(F32)
    diff = (lax.broadcasted_iota(jnp.int32, (c, c), 0)
            - lax.broadcasted_iota(jnp.int32, (c, c), 1)).astype(F32)
    dmask = jnp.where(diff >= 0, jnp.exp(lg * jnp.maximum(diff, 0.0)), 0.0)
    cross_dec = jnp.exp((li + 1.0) * lg)
    k_dec = jnp.exp((c - 1.0 - li) * lg)
    st_dec = jnp.exp(jnp.full((1, 1), c, F32) * lg)

    def body(i, state):
        off = pl.multiple_of(i * c, c)
        q = q_ref[pl.ds(off, c), :]
        k = k_ref[pl.ds(off, c), :]
        v = v_ref[pl.ds(off, c), :]
        s = _dot_nt(q, k) * dmask
        intra = jnp.dot(s.astype(BF16), v, preferred_element_type=F32)
        cross = jnp.dot(q, state.astype(BF16), preferred_element_type=F32) * cross_dec
        kd = (k.astype(F32) * k_dec).astype(BF16)
        new_state = state * st_dec + _dot_tn(kd, v)
        ret = _rms_rows(intra + cross) * _silu(g_ref[pl.ds(off, c), :])
        o_ref[pl.ds(off, c), :] = ret.astype(o_ref.dtype)
        return new_state

    st_ref[...] = lax.fori_loop(0, SEQ // c, body, jnp.zeros((RET_DK, RET_DV), F32))


def _prompt_retention(rq, rk, rv, g_ret):
    return pl.pallas_call(
        functools.partial(_ret_kernel, chunk=RET_CHUNK),
        out_shape=[jax.ShapeDtypeStruct((BATCH * SEQ, D_RET), BF16),
                   jax.ShapeDtypeStruct((BATCH, RET_HEADS, RET_DK, RET_DV), F32)],
        grid=(BATCH, RET_HEADS),
        in_specs=[pl.BlockSpec(memory_space=pltpu.SMEM),
                  pl.BlockSpec((SEQ, RET_DK), lambda b, h: (b, h)),
                  pl.BlockSpec((SEQ, RET_DK), lambda b, h: (b, h)),
                  pl.BlockSpec((SEQ, RET_DV), lambda b, h: (b, h)),
                  pl.BlockSpec((SEQ, RET_DV), lambda b, h: (b, h))],
        out_specs=[pl.BlockSpec((SEQ, RET_DV), lambda b, h: (b, h)),
                   pl.BlockSpec((None, None, RET_DK, RET_DV), lambda b, h: (b, h, 0, 0))],
        compiler_params=_cparams(2),
        name="prompt_retention",
    )(jnp.asarray(LOG_DECAY), rq, rk, rv, g_ret)


def _merge_kernel(a_ref, r_ref, w_ref, x_ref, g_ref, y_ref, *h_ref):
    ar = jnp.concatenate([a_ref[...], r_ref[...]], axis=-1)
    y = x_ref[...] + jnp.dot(ar, w_ref[...], preferred_element_type=F32)
    y_ref[...] = y
    if h_ref:
        h_ref[0][...] = (_rms_rows(y) * g_ref[...]).astype(BF16)


def _merge(a, r, w_out, layer, x, g_next, tm):
    t = x.shape[0]
    with_h = g_next is not None
    out_shape = [jax.ShapeDtypeStruct((t, D_MODEL), F32)]
    out_specs = [pl.BlockSpec((tm, D_MODEL), lambda m: (m, 0))]
    if with_h:
        out_shape.append(jax.ShapeDtypeStruct((t, D_MODEL), BF16))
        out_specs.append(pl.BlockSpec((tm, D_MODEL), lambda m: (m, 0)))
    else:
        g_next = jnp.ones((1, D_MODEL), F32)
    res = pl.pallas_call(
        _merge_kernel,
        out_shape=out_shape,
        grid=(t // tm,),
        in_specs=[pl.BlockSpec((tm, D_DA), lambda m: (m, 0)),
                  pl.BlockSpec((tm, D_RET), lambda m: (m, 0)),
                  pl.BlockSpec((None, D_MODEL, D_MODEL), lambda m: (layer, 0, 0)),
                  pl.BlockSpec((tm, D_MODEL), lambda m: (m, 0)),
                  pl.BlockSpec((1, D_MODEL), lambda m: (0, 0))],
        out_specs=out_specs,
        compiler_params=_cparams(1),
        name="merge",
    )(a, r, w_out, x, g_next)
    return (res[0], res[1]) if with_h else (res[0], None)


def _dec_attn_kernel(pt_ref, lq1_ref, lk1_ref, lq2_ref, lk2_ref, subg_ref,
                     q_ref, kn_ref, vn_ref, g_ref, *rest, lam_init, npg):
    del pt_ref
    k_refs = rest[:npg]
    v_refs = rest[npg:2 * npg]
    o_ref = rest[2 * npg]
    qm_ref, m_ref, l_ref, acc_ref = rest[2 * npg + 1:]
    p_idx = pl.program_id(1)
    rows = SAMPLE_ROWS

    @pl.when(p_idx == 0)
    def _():
        r = lax.broadcasted_iota(jnp.int32, (rows, D_DA), 0)
        cgrp = lax.broadcasted_iota(jnp.int32, (rows, D_DA), 1) // DA_HEAD_DIM
        qb = jnp.broadcast_to(q_ref[...], (rows, D_DA))
        qm_ref[...] = jnp.where(r == cgrp, qb, 0.0).astype(BF16)
        m_ref[...] = jnp.full(m_ref.shape, -jnp.inf, F32)
        l_ref[...] = jnp.zeros(l_ref.shape, F32)
        acc_ref[...] = jnp.zeros(acc_ref.shape, F32)

    def update(ks, vs, ntok_valid):
        qm = qm_ref[...]
        s = jnp.concatenate([_dot_nt(qm, k) for k in ks], axis=-1)
        if ntok_valid is not None:
            tcol = lax.broadcasted_iota(jnp.int32, s.shape, 1)
            s = jnp.where(tcol < ntok_valid, s, -jnp.inf)
        m_prev = m_ref[...]
        m_new = jnp.maximum(m_prev, jnp.max(s, axis=-1, keepdims=True))
        alpha = jnp.exp2(m_prev - m_new)
        p = jnp.exp2(s - m_new)
        l_ref[...] = alpha * l_ref[...] + jnp.sum(p, axis=-1, keepdims=True)
        pb = p.astype(BF16)
        ntok = s.shape[1] // len(ks)
        pv = jnp.dot(pb[:, :ntok], vs[0], preferred_element_type=F32)
        for i in range(1, len(ks)):
            pv = pv + jnp.dot(pb[:, i * ntok:(i + 1) * ntok], vs[i], preferred_element_type=F32)
        acc_ref[...] = alpha * acc_ref[...] + pv
        m_ref[...] = m_new

    def load_k(r):
        n_hc = 2 * DA_HEADS
        return jnp.concatenate(
            [r[pl.ds(hc, PAGE_SIZE, stride=n_hc), :].astype(BF16) for hc in range(n_hc)], axis=-1)

    def load_v(r):
        n_half = DA_DV // LANES
        return jnp.concatenate(
            [r[pl.ds(j * DA_HEADS + h, PAGE_SIZE, stride=n_half * DA_HEADS), :].astype(BF16)
             for h in range(DA_HEADS) for j in range(n_half)], axis=-1)

    update([load_k(r) for r in k_refs], [load_v(r) for r in v_refs], None)

    @pl.when(p_idx == pl.num_programs(1) - 1)
    def _():
        r = lax.broadcasted_iota(jnp.int32, (rows, D_DA), 0)
        kn = jnp.where(r == 0, jnp.broadcast_to(kn_ref[...], (rows, D_DA)), 0.0).astype(BF16)
        vn = jnp.where(r == 0, jnp.broadcast_to(vn_ref[...], (rows, D_DA)), 0.0).astype(BF16)
        update([kn], [vn], 1)
        lam = _lam_value(lq1_ref[...], lk1_ref[...], lq2_ref[...], lk2_ref[...], lam_init)
        o = acc_ref[...] / l_ref[...]
        g = g_ref[...]
        subg = subg_ref[...]
        for h in range(DA_HEADS):
            sl = slice(h * DA_DV, (h + 1) * DA_DV)
            att = o[2 * h:2 * h + 1, sl] - lam * o[2 * h + 1:2 * h + 2, sl]
            o_ref[:, sl] = _subln_gate(att, subg, g[:, sl], lam_init)


def _decode_attention(q, k_new, v_new, g_da, cache_k, cache_v, page_table, layer, lam_vecs, subg,
                      lam_init):
    npg = DEC_PAGES_PER_STEP
    vec = pl.BlockSpec((1, DA_HEAD_DIM), lambda b, p, pt: (0, 0))
    row = pl.BlockSpec((None, 1, D_DA), lambda b, p, pt: (b, 0, 0))

    def page_spec(i, trailing):
        return pl.BlockSpec((None,) + trailing,
                            lambda b, p, pt: (layer * N_POOL + pt[b, p * npg + i],) + (0,) * len(trailing))

    k_page = (PAGE_SIZE * DA_HEADS * 2, DA_HEAD_DIM)
    v_page = (PAGE_SIZE * DA_HEADS * (DA_DV // LANES), LANES)

    grid_spec = pltpu.PrefetchScalarGridSpec(
        num_scalar_prefetch=1,
        grid=(DEC_BATCH, N_PAGES // npg),
        in_specs=[vec, vec, vec, vec, pl.BlockSpec((1, DA_DV), lambda b, p, pt: (0, 0)),
                  row, row, row, row]
                 + [page_spec(i, k_page) for i in range(npg)]
                 + [page_spec(i, v_page) for i in range(npg)],
        out_specs=pl.BlockSpec((None, 1, D_DA), lambda b, p, pt: (b, 0, 0)),
        scratch_shapes=[pltpu.VMEM((SAMPLE_ROWS, D_DA), BF16),
                        pltpu.VMEM((SAMPLE_ROWS, 1), F32), pltpu.VMEM((SAMPLE_ROWS, 1), F32),
                        pltpu.VMEM((SAMPLE_ROWS, D_DA), F32)],
    )
    return pl.pallas_call(
        functools.partial(_dec_attn_kernel, lam_init=lam_init, npg=npg),
        out_shape=jax.ShapeDtypeStruct((DEC_BATCH, 1, D_DA), F32),
        grid_spec=grid_spec,
        compiler_params=_cparams(2),
        name="decode_attention",
    )(page_table, *lam_vecs, subg, q, k_new, v_new, g_da,
      *([cache_k] * npg), *([cache_v] * npg))


def _dec_ret_kernel(st_ref, qc_ref, kc_ref, v_ref, g_ref, nst_ref, o_ref):
    v_all = v_ref[...]
    g_all = g_ref[...]
    for h in range(RET_HEADS):
        dec = float(1.0 - 2.0 ** (-5.0 - h))
        sl = slice(h * RET_DV, (h + 1) * RET_DV)
        st = st_ref[h]
        qc = qc_ref[h]
        kc = kc_ref[h]
        v = v_all[:, sl]
        nst_ref[h] = st * dec + kc * v
        cross = jnp.sum(qc * st, axis=0, keepdims=True) * dec
        intra = jnp.sum(qc * kc, axis=0, keepdims=True) * v
        o_ref[:, sl] = _rms_rows(intra + cross) * _silu(g_all[:, sl])


def _decode_retention(state, rq_col, rk_col, rv, g_ret, layer):
    col = pl.BlockSpec((None, RET_HEADS, RET_DK, 1), lambda b: (b, 0, 0, 0))
    row = pl.BlockSpec((None, 1, D_RET), lambda b: (b, 0, 0))
    return pl.pallas_call(
        _dec_ret_kernel,
        out_shape=[jax.ShapeDtypeStruct((DEC_BATCH, RET_HEADS, RET_DK, RET_DV), F32),
                   jax.ShapeDtypeStruct((DEC_BATCH, 1, D_RET), F32)],
        grid=(DEC_BATCH,),
        in_specs=[pl.BlockSpec((None, None, RET_HEADS, RET_DK, RET_DV), lambda b: (layer, b, 0, 0, 0)),
                  col, col, row, row],
        out_specs=[pl.BlockSpec((None, RET_HEADS, RET_DK, RET_DV), lambda b: (b, 0, 0, 0)),
                   pl.BlockSpec((None, 1, D_RET), lambda b: (b, 0, 0))],
        compiler_params=_cparams(1),
        name="decode_retention",
    )(state, rq_col, rk_col, rv, g_ret)


def _rotary_tables(pos):
    half = RET_DK // 2
    theta = 1.0 / (ROPE_BASE ** jnp.linspace(0.0, 1.0, half, dtype=F32))
    ang = pos.astype(F32)[:, None] * theta[None, :]
    cos, sin = jnp.cos(ang), jnp.sin(ang)
    return jnp.concatenate([cos, cos], axis=-1), jnp.concatenate([-sin, sin], axis=-1)


def _pad_rows(x):
    return jnp.pad(x, ((0, SAMPLE_ROWS - x.shape[0]), (0, 0)))


def kernel(x_prompt, x_sample, cache_k, cache_v, state_ret, page_table, norm_g, w_in, w_out,
           qn_g, kn_g, lam_q1, lam_k1, lam_q2, lam_k2, subln_g):
    t_p = BATCH * SEQ
    w_in_bf = w_in.astype(BF16)
    w_out_bf = w_out.astype(BF16)
    cos_p, sin_p = _rotary_tables(jnp.arange(SEQ))
    cos_s, sin_s = _rotary_tables(jnp.full((SAMPLE_ROWS,), PAST_LEN))
    ck = cache_k.reshape(DEPTH * N_POOL, PAGE_SIZE * DA_HEADS * 2, DA_HEAD_DIM)
    cv = _values_to_cache_order(cache_v)

    xp = x_prompt.reshape(t_p, D_MODEL)
    xs = _pad_rows(x_sample.reshape(DEC_BATCH, D_MODEL))
    hp = _input_norm(xp, norm_g[0][None], PROMPT_TM)
    hs = _input_norm(xs, norm_g[0][None], SAMPLE_ROWS)

    kp_l, vp_l, sp_l, ks_l, vs_l, ss_l = [], [], [], [], [], []
    for l in range(DEPTH):
        lam_init = 0.8 - 0.6 * math.exp(-0.3 * l)
        lam_vecs = [a[l][None] for a in (lam_q1, lam_k1, lam_q2, lam_k2)]
        subg = subln_g[l][None]
        g_next = norm_g[l + 1][None] if l + 1 < DEPTH else None

        q, k32, kbf, v32, vbf, g_da, rq, rk, rv, g_ret = _project(
            hp, w_in_bf, l, qn_g[l][None], kn_g[l][None], cos_p, sin_p, PROMPT_TM, BF16)
        a = _prompt_attention(q, kbf, vbf, g_da, lam_vecs, subg, lam_init)
        r, st = _prompt_retention(rq, rk, rv, g_ret)
        xp, hp = _merge(a, r, w_out_bf, l, xp, g_next, MERGE_TM)
        kp_l.append(_keys_from_cache_order(k32, (BATCH, SEQ)))
        vp_l.append(_values_from_cache_order(v32, (BATCH, SEQ)))
        sp_l.append(st)

        q, k32, k_row, v32, v_row, g_da, rq, rk, rv, g_ret = _project(
            hs, w_in_bf, l, qn_g[l][None], kn_g[l][None], cos_s, sin_s, SAMPLE_ROWS, F32)
        as_row = lambda z: z[:DEC_BATCH].reshape(DEC_BATCH, 1, z.shape[-1])
        as_col = lambda z: z[:DEC_BATCH].reshape(DEC_BATCH, RET_HEADS, RET_DK, 1)
        a = _decode_attention(as_row(q), as_row(k_row), as_row(v_row), as_row(g_da), ck, cv, page_table,
                              l, lam_vecs, subg, lam_init)
        nst, r = _decode_retention(state_ret, as_col(rq), as_col(rk), as_row(rv), as_row(g_ret), l)
        a16 = _pad_rows(a.reshape(DEC_BATCH, D_DA)).astype(BF16)
        r16 = _pad_rows(r.reshape(DEC_BATCH, D_RET)).astype(BF16)
        xs, hs = _merge(a16, r16, w_out_bf, l, xs, g_next, SAMPLE_ROWS)
        ks_l.append(_keys_from_cache_order(k32, (SAMPLE_ROWS, DEC_SEQ))[:DEC_BATCH])
        vs_l.append(_values_from_cache_order(v32, (SAMPLE_ROWS, DEC_SEQ))[:DEC_BATCH])
        ss_l.append(nst)

    return (xp.reshape(BATCH, SEQ, D_MODEL),
            xs[:DEC_BATCH].reshape(DEC_BATCH, DEC_SEQ, D_MODEL),
            jnp.stack(kp_l), jnp.stack(vp_l), jnp.stack(sp_l),
            jnp.stack(ks_l), jnp.stack(vs_l), jnp.stack(ss_l))
```

```python
import functools
import math

import numpy as np
import jax
import jax.numpy as jnp
from jax import lax
from jax.experimental import pallas as pl
from jax.experimental.pallas import tpu as pltpu

D_MODEL = 2048
BATCH = 4
SEQ = 2048
DEPTH = 4
DEC_BATCH = 8
DEC_SEQ = 1
PAST_LEN = 16384
PAGE_SIZE = 128
N_PAGES = PAST_LEN // PAGE_SIZE
N_POOL = (DEC_BATCH * N_PAGES * 5) // 4

D_DA = D_MODEL // 2
D_RET = D_MODEL - D_DA
DA_HEAD_DIM = 128
DA_HEADS = D_DA // (2 * DA_HEAD_DIM)
DA_DV = 2 * DA_HEAD_DIM
RET_HEADS = 4
RET_DV = D_RET // RET_HEADS
RET_DK = RET_DV // 2
ROPE_BASE = 10000.0
EPS = 1e-6
D_IN = 4 * D_DA + 2 * RET_HEADS * RET_DK + 2 * D_RET

COL_Q, COL_K, COL_V, COL_GDA = 0, D_DA, 2 * D_DA, 3 * D_DA
COL_RQ = 4 * D_DA
COL_RK = COL_RQ + RET_HEADS * RET_DK
COL_RV = COL_RK + RET_HEADS * RET_DK
COL_GRET = COL_RV + D_RET

LANES = 128
SAMPLE_ROWS = 16
VMEM_LIMIT = 48 * 1024 * 1024

PROMPT_TM = 512
MERGE_TM = 256
ATTN_TQ = 512
Q_SCALE = DA_HEAD_DIM ** -0.5 * math.log2(math.e)
RET_CHUNK = 128
RET_TILE = 512
DEC_PAGES_PER_STEP = 8

LOG_DECAY = np.log(1.0 - 2.0 ** (-5.0 - np.arange(RET_HEADS, dtype=np.float32))).astype(np.float32)

F32 = jnp.float32
BF16 = jnp.bfloat16


def _cparams(n_axes):
    return pltpu.CompilerParams(dimension_semantics=("arbitrary",) * n_axes,
                                vmem_limit_bytes=VMEM_LIMIT)


def _silu(g):
    return g / (1.0 + jnp.exp(-g))


def _rms_rows(x):
    return x * lax.rsqrt(jnp.mean(x * x, axis=-1, keepdims=True) + EPS)


def _dot_nt(a, b):
    return lax.dot_general(a, b, (((1,), (1,)), ((), ())), preferred_element_type=F32)


def _dot_tn(a, b):
    return lax.dot_general(a, b, (((0,), (0,)), ((), ())), preferred_element_type=F32)


def _norm_kernel(x_ref, g_ref, h_ref):
    h_ref[...] = (_rms_rows(x_ref[...]) * g_ref[...]).astype(BF16)


def _input_norm(x, g, tm):
    t = x.shape[0]
    return pl.pallas_call(
        _norm_kernel,
        out_shape=jax.ShapeDtypeStruct((t, D_MODEL), BF16),
        grid=(t // tm,),
        in_specs=[pl.BlockSpec((tm, D_MODEL), lambda m: (m, 0)),
                  pl.BlockSpec((1, D_MODEL), lambda m: (0, 0))],
        out_specs=pl.BlockSpec((tm, D_MODEL), lambda m: (m, 0)),
        compiler_params=_cparams(1),
        name="input_norm",
    )(x, g)


def _proj_qk_kernel(h_ref, w_ref, g_ref, *out_refs, scale):
    z = jnp.dot(h_ref[...], w_ref[...], preferred_element_type=F32)
    tm = z.shape[0]
    g = g_ref[...]
    n_grp = D_DA // DA_HEAD_DIM
    for j in range(n_grp):
        sl = slice(j * DA_HEAD_DIM, (j + 1) * DA_HEAD_DIM)
        y = _rms_rows(z[:, sl]) * g
        if scale is not None:
            y = y * scale
        for o in out_refs:
            if o.shape[1] == DA_HEAD_DIM:
                o[pl.ds(j, tm, stride=n_grp), :] = y.astype(o.dtype)
            else:
                o[:, sl] = y.astype(o.dtype)


def _proj_v_kernel(h_ref, w_ref, v_ref, vact_ref):
    z = jnp.dot(h_ref[...], w_ref[...], preferred_element_type=F32)
    tm = z.shape[0]
    for h in range(DA_HEADS):
        for j in range(DA_DV // LANES):
            col = h * DA_DV + j * LANES
            v_ref[pl.ds(j * DA_HEADS + h, tm, stride=2 * DA_HEADS), :] = z[:, col:col + LANES]
    vact_ref[...] = z.astype(vact_ref.dtype)


def _proj_copy_kernel(h_ref, w_ref, *out_refs):
    z = jnp.dot(h_ref[...], w_ref[...], preferred_element_type=F32)
    for o in out_refs:
        o[...] = z.astype(o.dtype)


def _proj_rot_kernel(h_ref, w_ref, cos_ref, sin_ref, rq_ref, rk_ref):
    z = jnp.dot(h_ref[...], w_ref[...], preferred_element_type=F32)
    cos2 = cos_ref[...]
    sin2 = sin_ref[...]
    for j in range(2 * RET_HEADS):
        x = z[:, j * RET_DK:(j + 1) * RET_DK]
        if j >= RET_HEADS:
            x = x * (RET_DK ** -0.5)
        y = x * cos2 + pltpu.roll(x, RET_DK // 2, axis=1) * sin2
        if j < RET_HEADS:
            rq_ref[:, j * RET_DK:(j + 1) * RET_DK] = y.astype(rq_ref.dtype)
        else:
            jj = j - RET_HEADS
            rk_ref[:, jj * RET_DK:(jj + 1) * RET_DK] = y.astype(rk_ref.dtype)


def _with_carried_buffer(kern):
    def wrapped(carried_ref, *refs):
        del carried_ref
        kern(*refs)
    return wrapped


def _proj_call(kern, h, w_in, layer, col0, ncols, tm, extra, extra_specs, outs, name, stacked=None):
    t = h.shape[0]
    nm = t // tm
    assert col0 % ncols == 0
    cb = col0 // ncols
    out_shape = [jax.ShapeDtypeStruct((t * r, w), dt) for (r, w, dt) in outs]
    out_specs = [pl.BlockSpec((tm * r, w), lambda m: (m, 0)) for (r, w, _) in outs]
    inputs = [h, w_in, *extra]
    in_specs = [pl.BlockSpec((tm, D_MODEL), lambda m: (m, 0)),
                pl.BlockSpec((None, D_MODEL, ncols), lambda m: (layer, 0, cb))] + extra_specs
    aliases = {}
    if stacked is not None:
        i, buf = stacked
        r, w, dt = outs[i]
        out_shape[i] = jax.ShapeDtypeStruct((DEPTH * t * r, w), dt)
        out_specs[i] = pl.BlockSpec((tm * r, w), lambda m: (layer * nm + m, 0))
        if buf is not None:
            kern = _with_carried_buffer(kern)
            inputs = [buf] + inputs
            in_specs = [pl.BlockSpec(memory_space=pl.ANY)] + in_specs
            aliases = {0: i}
    return pl.pallas_call(
        kern,
        out_shape=out_shape,
        grid=(nm,),
        in_specs=in_specs,
        out_specs=out_specs,
        input_output_aliases=aliases,
        compiler_params=_cparams(1),
        name=name,
    )(*inputs)


def _project(h, w_in, layer, qn_g, kn_g, cos2, sin2, tm, act_dtype, kv_stacks=None):
    k_stacked = None if kv_stacks is None else (0, kv_stacks[0])
    v_stacked = None if kv_stacks is None else (0, kv_stacks[1])
    gspec = [pl.BlockSpec((1, DA_HEAD_DIM), lambda m: (0, 0))]
    cache_rows = 2 * DA_HEADS
    (q,) = _proj_call(functools.partial(_proj_qk_kernel, scale=Q_SCALE),
                      h, w_in, layer, COL_Q, D_DA, tm, [qn_g], gspec, [(1, D_DA, act_dtype)], "proj_q")
    k32, kact = _proj_call(functools.partial(_proj_qk_kernel, scale=None),
                           h, w_in, layer, COL_K, D_DA, tm, [kn_g], gspec,
                           [(cache_rows, DA_HEAD_DIM, F32), (1, D_DA, act_dtype)], "proj_k",
                           stacked=k_stacked)
    v32, vact = _proj_call(_proj_v_kernel, h, w_in, layer, COL_V, D_DA, tm, [], [],
                           [(cache_rows, LANES, F32), (1, D_DA, act_dtype)], "proj_v",
                           stacked=v_stacked)
    (g_da,) = _proj_call(_proj_copy_kernel, h, w_in, layer, COL_GDA, D_DA, tm, [], [],
                         [(1, D_DA, F32)], "proj_gda")
    n_tab = cos2.shape[0] // tm
    tspec = [pl.BlockSpec((tm, RET_DK), lambda m: (m % n_tab, 0))] * 2
    nrot = 2 * RET_HEADS * RET_DK
    rq, rk = _proj_call(_proj_rot_kernel, h, w_in, layer, COL_RQ, nrot, tm, [cos2, sin2], tspec,
                        [(1, nrot // 2, act_dtype), (1, nrot // 2, act_dtype)], "proj_rot")
    (rv,) = _proj_call(_proj_copy_kernel, h, w_in, layer, COL_RV, D_RET, tm, [], [],
                       [(1, D_RET, act_dtype)], "proj_rv")
    (g_ret,) = _proj_call(_proj_copy_kernel, h, w_in, layer, COL_GRET, D_RET, tm, [], [],
                          [(1, D_RET, F32)], "proj_gret")
    return q, k32, kact, v32, vact, g_da, rq, rk, rv, g_ret


def _keys_from_cache_order(k32, lead):
    return k32.reshape(lead + (DA_HEADS, 2, DA_HEAD_DIM))


def _values_from_cache_order(v32, lead):
    n_half = DA_DV // LANES
    v = v32.reshape(lead + (n_half, DA_HEADS, LANES))
    nd = len(lead)
    v = v.transpose(tuple(range(nd)) + (nd + 1, nd, nd + 2))
    return v.reshape(lead + (DA_HEADS, DA_DV))


def _values_to_cache_order(cache_v):
    n_half = DA_DV // LANES
    v = cache_v.reshape(-1, PAGE_SIZE, DA_HEADS, n_half, LANES).transpose(0, 1, 3, 2, 4)
    return v.reshape(-1, PAGE_SIZE * n_half * DA_HEADS, LANES)


def _lam_value(lq1, lk1, lq2, lk2, lam_init):
    s1 = jnp.sum(lq1 * lk1, axis=-1, keepdims=True)
    s2 = jnp.sum(lq2 * lk2, axis=-1, keepdims=True)
    return jnp.exp(s1) - jnp.exp(s2) + lam_init


def _subln_gate(o, subg, gate, lam_init):
    return (_rms_rows(o) * subg) * (1.0 - lam_init) * _silu(gate)


def _attn_kernel(lq1_ref, lk1_ref, lq2_ref, lk2_ref, subg_ref, q_ref, k_ref, v_ref, g_ref, o_ref,
                 m_ref, l_ref, acc_ref, *, lam_init, tq):
    qi = pl.program_id(2)
    q = q_ref[...]
    d = DA_HEAD_DIM

    m_ref[...] = jnp.full(m_ref.shape, -jnp.inf, F32)
    l_ref[...] = jnp.zeros(l_ref.shape, F32)
    acc_ref[...] = jnp.zeros(acc_ref.shape, F32)

    def step(k, v, mask):
        n_lt = tq // LANES
        upd = []
        for c in range(2):
            s = _dot_nt(q[:, c * d:(c + 1) * d], k[:, c * d:(c + 1) * d])
            if mask is not None:
                s = jnp.where(mask, s, -jnp.inf)
            st = [s[:, j * LANES:(j + 1) * LANES] for j in range(n_lt)]
            fold = st[0]
            for j in range(1, n_lt):
                fold = jnp.maximum(fold, st[j])
            m_prev = m_ref[c]
            m_new = jnp.maximum(m_prev, jnp.broadcast_to(jnp.max(fold, axis=-1, keepdims=True),
                                                         (tq, LANES)))
            alpha = jnp.exp2(m_prev - m_new)
            pt = [jnp.exp2(t - m_new) for t in st]
            psum = pt[0]
            for j in range(1, n_lt):
                psum = psum + pt[j]
            p = jnp.concatenate([t.astype(BF16) for t in pt], axis=-1)
            pv = jnp.dot(p, v, preferred_element_type=F32)
            upd.append((m_new, alpha, psum, pv))
        for c, (m_new, alpha, psum, pv) in enumerate(upd):
            m_ref[c] = m_new
            l_ref[c] = alpha * l_ref[c] + psum
            acc_ref[c] = jnp.concatenate([alpha] * (DA_DV // LANES), axis=-1) * acc_ref[c] + pv

    def body(ki, carry):
        off = pl.multiple_of(ki * tq, tq)
        step(k_ref[pl.ds(off, tq), :], v_ref[pl.ds(off, tq), :], None)
        return carry

    lax.fori_loop(0, qi, body, 0)

    off = pl.multiple_of(qi * tq, tq)
    row = lax.broadcasted_iota(jnp.int32, (tq, tq), 0)
    col = lax.broadcasted_iota(jnp.int32, (tq, tq), 1)
    step(k_ref[pl.ds(off, tq), :], v_ref[pl.ds(off, tq), :], col <= row)

    lam = _lam_value(lq1_ref[...], lk1_ref[...], lq2_ref[...], lk2_ref[...], lam_init)
    l1 = jnp.sum(l_ref[0], axis=-1, keepdims=True)
    l2 = jnp.sum(l_ref[1], axis=-1, keepdims=True)
    o = acc_ref[0] / l1 - lam * (acc_ref[1] / l2)
    o_ref[...] = _subln_gate(o, subg_ref[...], g_ref[...], lam_init).astype(o_ref.dtype)


def _prompt_attention(q, k, v, g_da, lam_vecs, subg, lam_init):
    tq = ATTN_TQ
    nq = SEQ // tq
    vec = pl.BlockSpec((1, DA_HEAD_DIM), lambda b, h, i: (0, 0))
    return pl.pallas_call(
        functools.partial(_attn_kernel, lam_init=lam_init, tq=tq),
        out_shape=jax.ShapeDtypeStruct((BATCH * SEQ, D_DA), BF16),
        grid=(BATCH, DA_HEADS, nq),
        in_specs=[vec, vec, vec, vec,
                  pl.BlockSpec((1, DA_DV), lambda b, h, i: (0, 0)),
                  pl.BlockSpec((tq, DA_DV), lambda b, h, i: (b * nq + i, h)),
                  pl.BlockSpec((SEQ, DA_DV), lambda b, h, i: (b, h)),
                  pl.BlockSpec((SEQ, DA_DV), lambda b, h, i: (b, h)),
                  pl.BlockSpec((tq, DA_DV), lambda b, h, i: (b * nq + i, h))],
        out_specs=pl.BlockSpec((tq, DA_DV), lambda b, h, i: (b * nq + i, h)),
        scratch_shapes=[pltpu.VMEM((2, tq, LANES), F32), pltpu.VMEM((2, tq, LANES), F32),
                        pltpu.VMEM((2, tq, DA_DV), F32)],
        compiler_params=_cparams(3),
        name="prompt_attention",
    )(*lam_vecs, subg, q, k, v, g_da)


def _ret_kernel(q_ref, k_ref, v_ref, g_ref, o_ref, st_ref, *, chunk):
    c = chunk
    n_chunks = q_ref.shape[0] // c

    @pl.when(pl.program_id(1) == 0)
    def _():
        st_ref[...] = jnp.zeros(st_ref.shape, F32)

    li = lax.broadcasted_iota(jnp.int32, (c, 1), 0).astype(F32)
    diff = (lax.broadcasted_iota(jnp.int32, (c, c), 0)
            - lax.broadcasted_iota(jnp.int32, (c, c), 1)).astype(F32)
    decays = []
    for h in range(RET_HEADS):
        lg = float(LOG_DECAY[h])
        decays.append((jnp.where(diff >= 0, jnp.exp(lg * jnp.maximum(diff, 0.0)), 0.0),
                       jnp.exp((li + 1.0) * lg), jnp.exp((c - 1.0 - li) * lg), math.exp(c * lg)))

    def body(i, carry):
        off = pl.multiple_of(i * c, c)
        for h in range(RET_HEADS):
            dmask, cross_dec, k_dec, st_dec = decays[h]
            q = q_ref[pl.ds(off, c), h * RET_DK:(h + 1) * RET_DK]
            k = k_ref[pl.ds(off, c), h * RET_DK:(h + 1) * RET_DK]
            v = v_ref[pl.ds(off, c), h * RET_DV:(h + 1) * RET_DV]
            state = st_ref[h]
            s = _dot_nt(q, k) * dmask
            intra = jnp.dot(s.astype(BF16), v, preferred_element_type=F32)
            cross = jnp.dot(q, state.astype(BF16), preferred_element_type=F32) * cross_dec
            kd = (k.astype(F32) * k_dec).astype(BF16)
            st_ref[h] = state * st_dec + _dot_tn(kd, v)
            gate = _silu(g_ref[pl.ds(off, c), h * RET_DV:(h + 1) * RET_DV])
            o_ref[pl.ds(off, c), h * RET_DV:(h + 1) * RET_DV] = (
                _rms_rows(intra + cross) * gate).astype(o_ref.dtype)
        return carry

    lax.fori_loop(0, n_chunks, body, 0)


def _prompt_retention(rq, rk, rv, g_ret):
    ts = RET_TILE
    ns = SEQ // ts
    nqk = RET_HEADS * RET_DK
    return pl.pallas_call(
        functools.partial(_ret_kernel, chunk=RET_CHUNK),
        out_shape=[jax.ShapeDtypeStruct((BATCH * SEQ, D_RET), BF16),
                   jax.ShapeDtypeStruct((BATCH, RET_HEADS, RET_DK, RET_DV), F32)],
        grid=(BATCH, ns),
        in_specs=[pl.BlockSpec((ts, nqk), lambda b, i: (b * ns + i, 0)),
                  pl.BlockSpec((ts, nqk), lambda b, i: (b * ns + i, 0)),
                  pl.BlockSpec((ts, D_RET), lambda b, i: (b * ns + i, 0)),
                  pl.BlockSpec((ts, D_RET), lambda b, i: (b * ns + i, 0))],
        out_specs=[pl.BlockSpec((ts, D_RET), lambda b, i: (b * ns + i, 0)),
                   pl.BlockSpec((None, RET_HEADS, RET_DK, RET_DV), lambda b, i: (b, 0, 0, 0))],
        compiler_params=_cparams(2),
        name="prompt_retention",
    )(rq, rk, rv, g_ret)


def _merge_kernel(a_ref, r_ref, w_ref, x_ref, g_ref, y_ref, *h_ref):
    ar = jnp.concatenate([a_ref[...], r_ref[...]], axis=-1)
    y = x_ref[...] + jnp.dot(ar, w_ref[...], preferred_element_type=F32)
    y_ref[...] = y
    if h_ref:
        h_ref[0][...] = (_rms_rows(y) * g_ref[...]).astype(BF16)


def _merge(a, r, w_out, layer, x, g_next, tm):
    t = x.shape[0]
    with_h = g_next is not None
    out_shape = [jax.ShapeDtypeStruct((t, D_MODEL), F32)]
    out_specs = [pl.BlockSpec((tm, D_MODEL), lambda m: (m, 0))]
    if with_h:
        out_shape.append(jax.ShapeDtypeStruct((t, D_MODEL), BF16))
        out_specs.append(pl.BlockSpec((tm, D_MODEL), lambda m: (m, 0)))
    else:
        g_next = jnp.ones((1, D_MODEL), F32)
    res = pl.pallas_call(
        _merge_kernel,
        out_shape=out_shape,
        grid=(t // tm,),
        in_specs=[pl.BlockSpec((tm, D_DA), lambda m: (m, 0)),
                  pl.BlockSpec((tm, D_RET), lambda m: (m, 0)),
                  pl.BlockSpec((None, D_MODEL, D_MODEL), lambda m: (layer, 0, 0)),
                  pl.BlockSpec((tm, D_MODEL), lambda m: (m, 0)),
                  pl.BlockSpec((1, D_MODEL), lambda m: (0, 0))],
        out_specs=out_specs,
        compiler_params=_cparams(1),
        name="merge",
    )(a, r, w_out, x, g_next)
    return (res[0], res[1]) if with_h else (res[0], None)


def _dec_attn_kernel(pt_ref, lq1_ref, lk1_ref, lq2_ref, lk2_ref, subg_ref,
                     q_ref, kn_ref, vn_ref, g_ref, *rest, lam_init, npg):
    del pt_ref
    k_refs = rest[:npg]
    v_refs = rest[npg:2 * npg]
    o_ref = rest[2 * npg]
    qm_ref, m_ref, l_ref, acc_ref = rest[2 * npg + 1:]
    p_idx = pl.program_id(1)
    rows = SAMPLE_ROWS

    @pl.when(p_idx == 0)
    def _():
        r = lax.broadcasted_iota(jnp.int32, (rows, D_DA), 0)
        cgrp = lax.broadcasted_iota(jnp.int32, (rows, D_DA), 1) // DA_HEAD_DIM
        qb = jnp.broadcast_to(q_ref[...], (rows, D_DA))
        qm_ref[...] = jnp.where(r == cgrp, qb, 0.0).astype(BF16)
        m_ref[...] = jnp.full(m_ref.shape, -jnp.inf, F32)
        l_ref[...] = jnp.zeros(l_ref.shape, F32)
        acc_ref[...] = jnp.zeros(acc_ref.shape, F32)

    def update(ks, vs, ntok_valid):
        qm = qm_ref[...]
        s = jnp.concatenate([_dot_nt(qm, k) for k in ks], axis=-1)
        if ntok_valid is not None:
            tcol = lax.broadcasted_iota(jnp.int32, s.shape, 1)
            s = jnp.where(tcol < ntok_valid, s, -jnp.inf)
        m_prev = m_ref[...]
        m_new = jnp.maximum(m_prev, jnp.max(s, axis=-1, keepdims=True))
        alpha = jnp.exp2(m_prev - m_new)
        p = jnp.exp2(s - m_new)
        l_ref[...] = alpha * l_ref[...] + jnp.sum(p, axis=-1, keepdims=True)
        pb = p.astype(BF16)
        ntok = s.shape[1] // len(ks)
        pv = jnp.dot(pb[:, :ntok], vs[0], preferred_element_type=F32)
        for i in range(1, len(ks)):
            pv = pv + jnp.dot(pb[:, i * ntok:(i + 1) * ntok], vs[i], preferred_element_type=F32)
        acc_ref[...] = alpha * acc_ref[...] + pv
        m_ref[...] = m_new

    def load_k(r):
        n_hc = 2 * DA_HEADS
        return jnp.concatenate(
            [r[pl.ds(hc, PAGE_SIZE, stride=n_hc), :].astype(BF16) for hc in range(n_hc)], axis=-1)

    def load_v(r):
        n_half = DA_DV // LANES
        return jnp.concatenate(
            [r[pl.ds(j * DA_HEADS + h, PAGE_SIZE, stride=n_half * DA_HEADS), :].astype(BF16)
             for h in range(DA_HEADS) for j in range(n_half)], axis=-1)

    update([load_k(r) for r in k_refs], [load_v(r) for r in v_refs], None)

    @pl.when(p_idx == pl.num_programs(1) - 1)
    def _():
        r = lax.broadcasted_iota(jnp.int32, (rows, D_DA), 0)
        kn = jnp.where(r == 0, jnp.broadcast_to(kn_ref[...], (rows, D_DA)), 0.0).astype(BF16)
        vn = jnp.where(r == 0, jnp.broadcast_to(vn_ref[...], (rows, D_DA)), 0.0).astype(BF16)
        update([kn], [vn], 1)
        lam = _lam_value(lq1_ref[...], lk1_ref[...], lq2_ref[...], lk2_ref[...], lam_init)
        o = acc_ref[...] / l_ref[...]
        g = g_ref[...]
        subg = subg_ref[...]
        for h in range(DA_HEADS):
            sl = slice(h * DA_DV, (h + 1) * DA_DV)
            att = o[2 * h:2 * h + 1, sl] - lam * o[2 * h + 1:2 * h + 2, sl]
            o_ref[:, sl] = _subln_gate(att, subg, g[:, sl], lam_init)


def _decode_attention(q, k_new, v_new, g_da, cache_k, cache_v, page_table, layer, lam_vecs, subg,
                      lam_init):
    npg = DEC_PAGES_PER_STEP
    vec = pl.BlockSpec((1, DA_HEAD_DIM), lambda b, p, pt: (0, 0))
    row = pl.BlockSpec((None, 1, D_DA), lambda b, p, pt: (b, 0, 0))

    def page_spec(i, trailing):
        return pl.BlockSpec((None,) + trailing,
                            lambda b, p, pt: (layer * N_POOL + pt[b, p * npg + i],) + (0,) * len(trailing))

    k_page = (PAGE_SIZE * DA_HEADS * 2, DA_HEAD_DIM)
    v_page = (PAGE_SIZE * DA_HEADS * (DA_DV // LANES), LANES)

    grid_spec = pltpu.PrefetchScalarGridSpec(
        num_scalar_prefetch=1,
        grid=(DEC_BATCH, N_PAGES // npg),
        in_specs=[vec, vec, vec, vec, pl.BlockSpec((1, DA_DV), lambda b, p, pt: (0, 0)),
                  row, row, row, row]
                 + [page_spec(i, k_page) for i in range(npg)]
                 + [page_spec(i, v_page) for i in range(npg)],
        out_specs=pl.BlockSpec((None, 1, D_DA), lambda b, p, pt: (b, 0, 0)),
        scratch_shapes=[pltpu.VMEM((SAMPLE_ROWS, D_DA), BF16),
                        pltpu.VMEM((SAMPLE_ROWS, 1), F32), pltpu.VMEM((SAMPLE_ROWS, 1), F32),
                        pltpu.VMEM((SAMPLE_ROWS, D_DA), F32)],
    )
    return pl.pallas_call(
        functools.partial(_dec_attn_kernel, lam_init=lam_init, npg=npg),
        out_shape=jax.ShapeDtypeStruct((DEC_BATCH, 1, D_DA), F32),
        grid_spec=grid_spec,
        compiler_params=_cparams(2),
        name="decode_attention",
    )(page_table, *lam_vecs, subg, q, k_new, v_new, g_da,
      *([cache_k] * npg), *([cache_v] * npg))


def _dec_ret_kernel(st_ref, qc_ref, kc_ref, v_ref, g_ref, nst_ref, o_ref):
    v_all = v_ref[...]
    g_all = g_ref[...]
    for h in range(RET_HEADS):
        dec = float(1.0 - 2.0 ** (-5.0 - h))
        sl = slice(h * RET_DV, (h + 1) * RET_DV)
        st = st_ref[h]
        qc = qc_ref[h]
        kc = kc_ref[h]
        v = v_all[:, sl]
        nst_ref[h] = st * dec + kc * v
        cross = jnp.sum(qc * st, axis=0, keepdims=True) * dec
        intra = jnp.sum(qc * kc, axis=0, keepdims=True) * v
        o_ref[:, sl] = _rms_rows(intra + cross) * _silu(g_all[:, sl])


def _decode_retention(state, rq_col, rk_col, rv, g_ret, layer):
    col = pl.BlockSpec((None, RET_HEADS, RET_DK, 1), lambda b: (b, 0, 0, 0))
    row = pl.BlockSpec((None, 1, D_RET), lambda b: (b, 0, 0))
    return pl.pallas_call(
        _dec_ret_kernel,
        out_shape=[jax.ShapeDtypeStruct((DEC_BATCH, RET_HEADS, RET_DK, RET_DV), F32),
                   jax.ShapeDtypeStruct((DEC_BATCH, 1, D_RET), F32)],
        grid=(DEC_BATCH,),
        in_specs=[pl.BlockSpec((None, None, RET_HEADS, RET_DK, RET_DV), lambda b: (layer, b, 0, 0, 0)),
                  col, col, row, row],
        out_specs=[pl.BlockSpec((None, RET_HEADS, RET_DK, RET_DV), lambda b: (b, 0, 0, 0)),
                   pl.BlockSpec((None, 1, D_RET), lambda b: (b, 0, 0))],
        compiler_params=_cparams(1),
        name="decode_retention",
    )(state, rq_col, rk_col, rv, g_ret)


def _rotary_tables(pos):
    half = RET_DK // 2
    theta = 1.0 / (ROPE_BASE ** jnp.linspace(0.0, 1.0, half, dtype=F32))
    ang = pos.astype(F32)[:, None] * theta[None, :]
    cos, sin = jnp.cos(ang), jnp.sin(ang)
    return jnp.concatenate([cos, cos], axis=-1), jnp.concatenate([-sin, sin], axis=-1)


def _pad_rows(x):
    return jnp.pad(x, ((0, SAMPLE_ROWS - x.shape[0]), (0, 0)))


def kernel(x_prompt, x_sample, cache_k, cache_v, state_ret, page_table, norm_g, w_in, w_out,
           qn_g, kn_g, lam_q1, lam_k1, lam_q2, lam_k2, subln_g):
    t_p = BATCH * SEQ
    w_in_bf = w_in.astype(BF16)
    w_out_bf = w_out.astype(BF16)
    cos_p, sin_p = _rotary_tables(jnp.arange(SEQ))
    cos_s, sin_s = _rotary_tables(jnp.full((SAMPLE_ROWS,), PAST_LEN))
    ck = cache_k.reshape(DEPTH * N_POOL, PAGE_SIZE * DA_HEADS * 2, DA_HEAD_DIM)
    cv = _values_to_cache_order(cache_v)

    xp = x_prompt.reshape(t_p, D_MODEL)
    xs = _pad_rows(x_sample.reshape(DEC_BATCH, D_MODEL))
    hp = _input_norm(xp, norm_g[0][None], PROMPT_TM)
    hs = _input_norm(xs, norm_g[0][None], SAMPLE_ROWS)

    kp_all = vp_all = None
    sp_l, ks_l, vs_l, ss_l = [], [], [], []
    for l in range(DEPTH):
        lam_init = 0.8 - 0.6 * math.exp(-0.3 * l)
        lam_vecs = [a[l][None] for a in (lam_q1, lam_k1, lam_q2, lam_k2)]
        subg = subln_g[l][None]
        g_next = norm_g[l + 1][None] if l + 1 < DEPTH else None

        q, kp_all, kbf, vp_all, vbf, g_da, rq, rk, rv, g_ret = _project(
            hp, w_in_bf, l, qn_g[l][None], kn_g[l][None], cos_p, sin_p, PROMPT_TM, BF16,
            kv_stacks=(kp_all, vp_all))
        a = _prompt_attention(q, kbf, vbf, g_da, lam_vecs, subg, lam_init)
        r, st = _prompt_retention(rq, rk, rv, g_ret)
        xp, hp = _merge(a, r, w_out_bf, l, xp, g_next, MERGE_TM)
        sp_l.append(st)

        q, k32, k_row, v32, v_row, g_da, rq, rk, rv, g_ret = _project(
            hs, w_in_bf, l, qn_g[l][None], kn_g[l][None], cos_s, sin_s, SAMPLE_ROWS, F32)
        as_row = lambda z: z[:DEC_BATCH].reshape(DEC_BATCH, 1, z.shape[-1])
        as_col = lambda z: z[:DEC_BATCH].reshape(DEC_BATCH, RET_HEADS, RET_DK, 1)
        a = _decode_attention(as_row(q), as_row(k_row), as_row(v_row), as_row(g_da), ck, cv, page_table,
                              l, lam_vecs, subg, lam_init)
        nst, r = _decode_retention(state_ret, as_col(rq), as_col(rk), as_row(rv), as_row(g_ret), l)
        a16 = _pad_rows(a.reshape(DEC_BATCH, D_DA)).astype(BF16)
        r16 = _pad_rows(r.reshape(DEC_BATCH, D_RET)).astype(BF16)
        xs, hs = _merge(a16, r16, w_out_bf, l, xs, g_next, SAMPLE_ROWS)
        ks_l.append(_keys_from_cache_order(k32, (SAMPLE_ROWS, DEC_SEQ))[:DEC_BATCH])
        vs_l.append(_values_from_cache_order(v32, (SAMPLE_ROWS, DEC_SEQ))[:DEC_BATCH])
        ss_l.append(nst)

    return (xp.reshape(BATCH, SEQ, D_MODEL),
            xs[:DEC_BATCH].reshape(DEC_BATCH, DEC_SEQ, D_MODEL),
            _keys_from_cache_order(kp_all, (DEPTH, BATCH, SEQ)),
            _values_from_cache_order(vp_all, (DEPTH, BATCH, SEQ)), jnp.stack(sp_l),
            jnp.stack(ks_l), jnp.stack(vs_l), jnp.stack(ss_l))
```

```python
import functools
import math
from typing import NamedTuple

import numpy as np
import jax
import jax.numpy as jnp
from jax import lax
from jax.experimental import pallas as pl
from jax.experimental.pallas import tpu as pltpu

D_MODEL = 2048
BATCH = 4
SEQ = 2048
DEPTH = 4
DEC_BATCH = 8
DEC_SEQ = 1
PAST_LEN = 16384
PAGE_SIZE = 128
N_PAGES = PAST_LEN // PAGE_SIZE
N_POOL = (DEC_BATCH * N_PAGES * 5) // 4

D_DA = D_MODEL // 2
D_RET = D_MODEL - D_DA
DA_HEAD_DIM = 128
DA_HEADS = D_DA // (2 * DA_HEAD_DIM)
DA_DV = 2 * DA_HEAD_DIM
RET_HEADS = 4
RET_DV = D_RET // RET_HEADS
RET_DK = RET_DV // 2
ROPE_BASE = 10000.0
EPS = 1e-6
D_IN = 4 * D_DA + 2 * RET_HEADS * RET_DK + 2 * D_RET

COL_Q, COL_K, COL_V, COL_GDA = 0, D_DA, 2 * D_DA, 3 * D_DA
COL_RQ = 4 * D_DA
COL_RK = COL_RQ + RET_HEADS * RET_DK
COL_RV = COL_RK + RET_HEADS * RET_DK
COL_GRET = COL_RV + D_RET

LANES = 128
SAMPLE_ROWS = 16
VMEM_LIMIT = 48 * 1024 * 1024

PROMPT_TM = 512
MERGE_TM = 256
ATTN_TQ = 512
Q_SCALE = DA_HEAD_DIM ** -0.5 * math.log2(math.e)
RET_CHUNK = 128
RET_TILE = 512

LOG_DECAY = np.log(1.0 - 2.0 ** (-5.0 - np.arange(RET_HEADS, dtype=np.float32))).astype(np.float32)

F32 = jnp.float32
BF16 = jnp.bfloat16


def _cparams(n_axes):
    return pltpu.CompilerParams(dimension_semantics=("arbitrary",) * n_axes,
                                vmem_limit_bytes=VMEM_LIMIT)


def _silu(g):
    return g / (1.0 + jnp.exp(-g))


def _rms_rows(x):
    return x * lax.rsqrt(jnp.mean(x * x, axis=-1, keepdims=True) + EPS)


def _dot_nt(a, b):
    return lax.dot_general(a, b, (((1,), (1,)), ((), ())), preferred_element_type=F32)


def _dot_tn(a, b):
    return lax.dot_general(a, b, (((0,), (0,)), ((), ())), preferred_element_type=F32)


def _norm_kernel(x_ref, g_ref, h_ref):
    h_ref[...] = (_rms_rows(x_ref[...]) * g_ref[...]).astype(BF16)


def _input_norm(x, g, tm):
    t = x.shape[0]
    return pl.pallas_call(
        _norm_kernel,
        out_shape=jax.ShapeDtypeStruct((t, D_MODEL), BF16),
        grid=(t // tm,),
        in_specs=[pl.BlockSpec((tm, D_MODEL), lambda m: (m, 0)),
                  pl.BlockSpec((1, D_MODEL), lambda m: (0, 0))],
        out_specs=pl.BlockSpec((tm, D_MODEL), lambda m: (m, 0)),
        compiler_params=_cparams(1),
        name="input_norm",
    )(x, g)


def _proj_qk_kernel(h_ref, w_ref, g_ref, *out_refs, scale):
    z = jnp.dot(h_ref[...], w_ref[...], preferred_element_type=F32)
    tm = z.shape[0]
    g = g_ref[...]
    n_grp = D_DA // DA_HEAD_DIM
    for j in range(n_grp):
        sl = slice(j * DA_HEAD_DIM, (j + 1) * DA_HEAD_DIM)
        y = _rms_rows(z[:, sl]) * g
        if scale is not None:
            y = y * scale
        for o in out_refs:
            if o.shape[1] == DA_HEAD_DIM:
                o[pl.ds(j, tm, stride=n_grp), :] = y.astype(o.dtype)
            else:
                o[:, sl] = y.astype(o.dtype)


def _proj_v_kernel(h_ref, w_ref, v_ref, vact_ref):
    z = jnp.dot(h_ref[...], w_ref[...], preferred_element_type=F32)
    tm = z.shape[0]
    for h in range(DA_HEADS):
        for j in range(DA_DV // LANES):
            col = h * DA_DV + j * LANES
            v_ref[pl.ds(j * DA_HEADS + h, tm, stride=2 * DA_HEADS), :] = z[:, col:col + LANES]
    vact_ref[...] = z.astype(vact_ref.dtype)


def _proj_copy_kernel(h_ref, w_ref, *out_refs):
    z = jnp.dot(h_ref[...], w_ref[...], preferred_element_type=F32)
    for o in out_refs:
        o[...] = z.astype(o.dtype)


def _proj_rot_kernel(h_ref, w_ref, cos_ref, sin_ref, rq_ref, rk_ref):
    z = jnp.dot(h_ref[...], w_ref[...], preferred_element_type=F32)
    cos2 = cos_ref[...]
    sin2 = sin_ref[...]
    for j in range(2 * RET_HEADS):
        x = z[:, j * RET_DK:(j + 1) * RET_DK]
        if j >= RET_HEADS:
            x = x * (RET_DK ** -0.5)
        y = x * cos2 + pltpu.roll(x, RET_DK // 2, axis=1) * sin2
        if j < RET_HEADS:
            rq_ref[:, j * RET_DK:(j + 1) * RET_DK] = y.astype(rq_ref.dtype)
        else:
            jj = j - RET_HEADS
            rk_ref[:, jj * RET_DK:(jj + 1) * RET_DK] = y.astype(rk_ref.dtype)


def _with_carried_buffer(kern):
    def wrapped(carried_ref, *refs):
        del carried_ref
        kern(*refs)
    return wrapped


def _with_decode(proj_kern, dec_kern, n_proj_in, n_dec_in, n_proj_out):
    def wrapped(pt_ref, *refs):
        proj_in = refs[:n_proj_in]
        dec_in = refs[n_proj_in:n_proj_in + n_dec_in]
        outs = refs[n_proj_in + n_dec_in:]
        dec_kern(pt_ref, *dec_in, *outs[n_proj_out:],
                 co_body=functools.partial(proj_kern, *proj_in, *outs[:n_proj_out]))
    return wrapped


def _proj_call(kern, h, w_in, layer, col0, ncols, tm, extra, extra_specs, outs, name, stacked=None,
               decode=None):
    t = h.shape[0]
    nm = t // tm
    assert col0 % ncols == 0
    cb = col0 // ncols
    out_shape = [jax.ShapeDtypeStruct((t * r, w), dt) for (r, w, dt) in outs]
    out_specs = [pl.BlockSpec((tm * r, w), lambda m, *_: (m, 0)) for (r, w, _) in outs]
    inputs = [h, w_in, *extra]
    in_specs = [pl.BlockSpec((tm, D_MODEL), lambda m, *_: (m, 0)),
                pl.BlockSpec((None, D_MODEL, ncols), lambda m, *_: (layer, 0, cb))] + extra_specs
    aliases = {}
    if stacked is not None:
        i, buf = stacked
        r, w, dt = outs[i]
        out_shape[i] = jax.ShapeDtypeStruct((DEPTH * t * r, w), dt)
        out_specs[i] = pl.BlockSpec((tm * r, w), lambda m, *_: (layer * nm + m, 0))
        if buf is not None:
            kern = _with_carried_buffer(kern)
            inputs = [buf] + inputs
            in_specs = [pl.BlockSpec(memory_space=pl.ANY)] + in_specs
            aliases = {0: i}
    return _launch_rows(kern, nm, inputs, in_specs, out_shape, out_specs, aliases, name, decode)


def _launch_rows(kern, nm, inputs, in_specs, out_shape, out_specs, aliases, name, decode):
    if decode is None:
        return pl.pallas_call(
            kern,
            out_shape=out_shape,
            grid=(nm,),
            in_specs=in_specs,
            out_specs=out_specs,
            input_output_aliases=aliases,
            compiler_params=_cparams(1),
            name=name,
        )(*inputs)

    job, b = decode
    npg = N_PAGES // nm
    assert npg * nm == N_PAGES
    dec = _decode_operands(job, npg, b, 1, lambda idx: (b, idx[0]))
    body = _with_decode(kern, functools.partial(_dec_attn_kernel, lam_init=job.lam_init, npg=npg, step_axis=0),
                        len(inputs), len(dec.inputs), len(out_shape))
    grid_spec = pltpu.PrefetchScalarGridSpec(
        num_scalar_prefetch=1,
        grid=(nm,),
        in_specs=in_specs + dec.in_specs,
        out_specs=out_specs + [dec.out_spec],
        scratch_shapes=dec.scratch_shapes,
    )
    return pl.pallas_call(
        body,
        out_shape=out_shape + [dec.out_shape],
        grid_spec=grid_spec,
        input_output_aliases={k + 1: v for k, v in aliases.items()},
        compiler_params=_cparams(1),
        name=name + "_dec",
    )(job.page_table, *inputs, *dec.inputs)


N_PROJ_CALLS = 7


def _project(h, w_in, layer, qn_g, kn_g, cos2, sin2, tm, act_dtype, kv_stacks=None, decode_job=None):
    k_stacked = None if kv_stacks is None else (0, kv_stacks[0])
    v_stacked = None if kv_stacks is None else (0, kv_stacks[1])
    gspec = [pl.BlockSpec((1, DA_HEAD_DIM), lambda m, *_: (0, 0))]
    cache_rows = 2 * DA_HEADS
    dec_rows = []

    def call(kern, col0, ncols, extra, extra_specs, outs, name, stacked=None):
        decode = None if decode_job is None else (decode_job, len(dec_rows))
        res = _proj_call(kern, h, w_in, layer, col0, ncols, tm, extra, extra_specs, outs, name,
                         stacked=stacked, decode=decode)
        if decode is not None:
            dec_rows.append(res[-1])
            res = res[:-1]
        return res

    (q,) = call(functools.partial(_proj_qk_kernel, scale=Q_SCALE), COL_Q, D_DA, [qn_g], gspec,
                [(1, D_DA, act_dtype)], "proj_q")
    k32, kact = call(functools.partial(_proj_qk_kernel, scale=None), COL_K, D_DA, [kn_g], gspec,
                     [(cache_rows, DA_HEAD_DIM, F32), (1, D_DA, act_dtype)], "proj_k", stacked=k_stacked)
    v32, vact = call(_proj_v_kernel, COL_V, D_DA, [], [],
                     [(cache_rows, LANES, F32), (1, D_DA, act_dtype)], "proj_v", stacked=v_stacked)
    (g_da,) = call(_proj_copy_kernel, COL_GDA, D_DA, [], [], [(1, D_DA, F32)], "proj_gda")
    n_tab = cos2.shape[0] // tm
    tspec = [pl.BlockSpec((tm, RET_DK), lambda m, *_: (m % n_tab, 0))] * 2
    nrot = 2 * RET_HEADS * RET_DK
    rq, rk = call(_proj_rot_kernel, COL_RQ, nrot, [cos2, sin2], tspec,
                  [(1, nrot // 2, act_dtype), (1, nrot // 2, act_dtype)], "proj_rot")
    (rv,) = call(_proj_copy_kernel, COL_RV, D_RET, [], [], [(1, D_RET, act_dtype)], "proj_rv")
    (g_ret,) = call(_proj_copy_kernel, COL_GRET, D_RET, [], [], [(1, D_RET, F32)], "proj_gret")
    assert decode_job is None or len(dec_rows) == N_PROJ_CALLS
    return q, k32, kact, v32, vact, g_da, rq, rk, rv, g_ret, dec_rows


def _keys_from_cache_order(k32, lead):
    return k32.reshape(lead + (DA_HEADS, 2, DA_HEAD_DIM))


def _values_from_cache_order(v32, lead):
    n_half = DA_DV // LANES
    v = v32.reshape(lead + (n_half, DA_HEADS, LANES))
    nd = len(lead)
    v = v.transpose(tuple(range(nd)) + (nd + 1, nd, nd + 2))
    return v.reshape(lead + (DA_HEADS, DA_DV))


def _values_to_cache_order(cache_v):
    n_half = DA_DV // LANES
    v = cache_v.reshape(-1, PAGE_SIZE, DA_HEADS, n_half, LANES).transpose(0, 1, 3, 2, 4)
    return v.reshape(-1, PAGE_SIZE * n_half * DA_HEADS, LANES)


def _lam_value(lq1, lk1, lq2, lk2, lam_init):
    s1 = jnp.sum(lq1 * lk1, axis=-1, keepdims=True)
    s2 = jnp.sum(lq2 * lk2, axis=-1, keepdims=True)
    return jnp.exp(s1) - jnp.exp(s2) + lam_init


def _subln_gate(o, subg, gate, lam_init):
    return (_rms_rows(o) * subg) * (1.0 - lam_init) * _silu(gate)


def _attn_kernel(lq1_ref, lk1_ref, lq2_ref, lk2_ref, subg_ref, q_ref, k_ref, v_ref, g_ref, o_ref,
                 m_ref, l_ref, acc_ref, *, lam_init, tq):
    qi = pl.program_id(2)
    q = q_ref[...]
    d = DA_HEAD_DIM

    m_ref[...] = jnp.full(m_ref.shape, -jnp.inf, F32)
    l_ref[...] = jnp.zeros(l_ref.shape, F32)
    acc_ref[...] = jnp.zeros(acc_ref.shape, F32)

    def step(k, v, mask):
        n_lt = tq // LANES
        upd = []
        for c in range(2):
            s = _dot_nt(q[:, c * d:(c + 1) * d], k[:, c * d:(c + 1) * d])
            if mask is not None:
                s = jnp.where(mask, s, -jnp.inf)
            st = [s[:, j * LANES:(j + 1) * LANES] for j in range(n_lt)]
            fold = st[0]
            for j in range(1, n_lt):
                fold = jnp.maximum(fold, st[j])
            m_prev = m_ref[c]
            m_new = jnp.maximum(m_prev, jnp.broadcast_to(jnp.max(fold, axis=-1, keepdims=True),
                                                         (tq, LANES)))
            alpha = jnp.exp2(m_prev - m_new)
            pt = [jnp.exp2(t - m_new) for t in st]
            psum = pt[0]
            for j in range(1, n_lt):
                psum = psum + pt[j]
            p = jnp.concatenate([t.astype(BF16) for t in pt], axis=-1)
            pv = jnp.dot(p, v, preferred_element_type=F32)
            upd.append((m_new, alpha, psum, pv))
        for c, (m_new, alpha, psum, pv) in enumerate(upd):
            m_ref[c] = m_new
            l_ref[c] = alpha * l_ref[c] + psum
            acc_ref[c] = jnp.concatenate([alpha] * (DA_DV // LANES), axis=-1) * acc_ref[c] + pv

    def body(ki, carry):
        off = pl.multiple_of(ki * tq, tq)
        step(k_ref[pl.ds(off, tq), :], v_ref[pl.ds(off, tq), :], None)
        return carry

    lax.fori_loop(0, qi, body, 0)

    off = pl.multiple_of(qi * tq, tq)
    row = lax.broadcasted_iota(jnp.int32, (tq, tq), 0)
    col = lax.broadcasted_iota(jnp.int32, (tq, tq), 1)
    step(k_ref[pl.ds(off, tq), :], v_ref[pl.ds(off, tq), :], col <= row)

    lam = _lam_value(lq1_ref[...], lk1_ref[...], lq2_ref[...], lk2_ref[...], lam_init)
    l1 = jnp.sum(l_ref[0], axis=-1, keepdims=True)
    l2 = jnp.sum(l_ref[1], axis=-1, keepdims=True)
    o = acc_ref[0] / l1 - lam * (acc_ref[1] / l2)
    o_ref[...] = _subln_gate(o, subg_ref[...], g_ref[...], lam_init).astype(o_ref.dtype)


def _prompt_attention(q, k, v, g_da, lam_vecs, subg, lam_init):
    tq = ATTN_TQ
    nq = SEQ // tq
    vec = pl.BlockSpec((1, DA_HEAD_DIM), lambda b, h, i: (0, 0))
    return pl.pallas_call(
        functools.partial(_attn_kernel, lam_init=lam_init, tq=tq),
        out_shape=jax.ShapeDtypeStruct((BATCH * SEQ, D_DA), BF16),
        grid=(BATCH, DA_HEADS, nq),
        in_specs=[vec, vec, vec, vec,
                  pl.BlockSpec((1, DA_DV), lambda b, h, i: (0, 0)),
                  pl.BlockSpec((tq, DA_DV), lambda b, h, i: (b * nq + i, h)),
                  pl.BlockSpec((SEQ, DA_DV), lambda b, h, i: (b, h)),
                  pl.BlockSpec((SEQ, DA_DV), lambda b, h, i: (b, h)),
                  pl.BlockSpec((tq, DA_DV), lambda b, h, i: (b * nq + i, h))],
        out_specs=pl.BlockSpec((tq, DA_DV), lambda b, h, i: (b * nq + i, h)),
        scratch_shapes=[pltpu.VMEM((2, tq, LANES), F32), pltpu.VMEM((2, tq, LANES), F32),
                        pltpu.VMEM((2, tq, DA_DV), F32)],
        compiler_params=_cparams(3),
        name="prompt_attention",
    )(*lam_vecs, subg, q, k, v, g_da)


def _ret_kernel(q_ref, k_ref, v_ref, g_ref, o_ref, st_ref, *, chunk):
    c = chunk
    n_chunks = q_ref.shape[0] // c

    @pl.when(pl.program_id(1) == 0)
    def _():
        st_ref[...] = jnp.zeros(st_ref.shape, F32)

    li = lax.broadcasted_iota(jnp.int32, (c, 1), 0).astype(F32)
    diff = (lax.broadcasted_iota(jnp.int32, (c, c), 0)
            - lax.broadcasted_iota(jnp.int32, (c, c), 1)).astype(F32)
    decays = []
    for h in range(RET_HEADS):
        lg = float(LOG_DECAY[h])
        decays.append((jnp.where(diff >= 0, jnp.exp(lg * jnp.maximum(diff, 0.0)), 0.0),
                       jnp.exp((li + 1.0) * lg), jnp.exp((c - 1.0 - li) * lg), math.exp(c * lg)))

    def body(i, carry):
        off = pl.multiple_of(i * c, c)
        for h in range(RET_HEADS):
            dmask, cross_dec, k_dec, st_dec = decays[h]
            q = q_ref[pl.ds(off, c), h * RET_DK:(h + 1) * RET_DK]
            k = k_ref[pl.ds(off, c), h * RET_DK:(h + 1) * RET_DK]
            v = v_ref[pl.ds(off, c), h * RET_DV:(h + 1) * RET_DV]
            state = st_ref[h]
            s = _dot_nt(q, k) * dmask
            intra = jnp.dot(s.astype(BF16), v, preferred_element_type=F32)
            cross = jnp.dot(q, state.astype(BF16), preferred_element_type=F32) * cross_dec
            kd = (k.astype(F32) * k_dec).astype(BF16)
            st_ref[h] = state * st_dec + _dot_tn(kd, v)
            gate = _silu(g_ref[pl.ds(off, c), h * RET_DV:(h + 1) * RET_DV])
            o_ref[pl.ds(off, c), h * RET_DV:(h + 1) * RET_DV] = (
                _rms_rows(intra + cross) * gate).astype(o_ref.dtype)
        return carry

    lax.fori_loop(0, n_chunks, body, 0)


def _prompt_retention(rq, rk, rv, g_ret):
    ts = RET_TILE
    ns = SEQ // ts
    nqk = RET_HEADS * RET_DK
    return pl.pallas_call(
        functools.partial(_ret_kernel, chunk=RET_CHUNK),
        out_shape=[jax.ShapeDtypeStruct((BATCH * SEQ, D_RET), BF16),
                   jax.ShapeDtypeStruct((BATCH, RET_HEADS, RET_DK, RET_DV), F32)],
        grid=(BATCH, ns),
        in_specs=[pl.BlockSpec((ts, nqk), lambda b, i: (b * ns + i, 0)),
                  pl.BlockSpec((ts, nqk), lambda b, i: (b * ns + i, 0)),
                  pl.BlockSpec((ts, D_RET), lambda b, i: (b * ns + i, 0)),
                  pl.BlockSpec((ts, D_RET), lambda b, i: (b * ns + i, 0))],
        out_specs=[pl.BlockSpec((ts, D_RET), lambda b, i: (b * ns + i, 0)),
                   pl.BlockSpec((None, RET_HEADS, RET_DK, RET_DV), lambda b, i: (b, 0, 0, 0))],
        compiler_params=_cparams(2),
        name="prompt_retention",
    )(rq, rk, rv, g_ret)


def _merge_kernel(a_ref, r_ref, w_ref, x_ref, g_ref, y_ref, *h_ref):
    ar = jnp.concatenate([a_ref[...], r_ref[...]], axis=-1)
    y = x_ref[...] + jnp.dot(ar, w_ref[...], preferred_element_type=F32)
    y_ref[...] = y
    if h_ref:
        h_ref[0][...] = (_rms_rows(y) * g_ref[...]).astype(BF16)


def _merge(a, r, w_out, layer, x, g_next, tm, decode=None):
    t = x.shape[0]
    with_h = g_next is not None
    out_shape = [jax.ShapeDtypeStruct((t, D_MODEL), F32)]
    out_specs = [pl.BlockSpec((tm, D_MODEL), lambda m, *_: (m, 0))]
    if with_h:
        out_shape.append(jax.ShapeDtypeStruct((t, D_MODEL), BF16))
        out_specs.append(pl.BlockSpec((tm, D_MODEL), lambda m, *_: (m, 0)))
    else:
        g_next = jnp.ones((1, D_MODEL), F32)
    in_specs = [pl.BlockSpec((tm, D_DA), lambda m, *_: (m, 0)),
                pl.BlockSpec((tm, D_RET), lambda m, *_: (m, 0)),
                pl.BlockSpec((None, D_MODEL, D_MODEL), lambda m, *_: (layer, 0, 0)),
                pl.BlockSpec((tm, D_MODEL), lambda m, *_: (m, 0)),
                pl.BlockSpec((1, D_MODEL), lambda m, *_: (0, 0))]
    res = list(_launch_rows(_merge_kernel, t // tm, [a, r, w_out, x, g_next], in_specs, out_shape, out_specs,
                            {}, "merge", decode))
    dec_row = res.pop() if decode is not None else None
    return res[0], (res[1] if with_h else None), dec_row


def _dec_attn_kernel(pt_ref, lq1_ref, lk1_ref, lq2_ref, lk2_ref, subg_ref,
                     q_ref, kn_ref, vn_ref, g_ref, *rest, lam_init, npg, step_axis, co_body=None):
    del pt_ref
    k_refs = rest[:npg]
    v_refs = rest[npg:2 * npg]
    o_ref = rest[2 * npg]
    qm_ref, m_ref, l_ref, acc_ref = rest[2 * npg + 1:]
    p_idx = pl.program_id(step_axis)
    rows = SAMPLE_ROWS

    @pl.when(p_idx == 0)
    def _():
        r = lax.broadcasted_iota(jnp.int32, (rows, D_DA), 0)
        cgrp = lax.broadcasted_iota(jnp.int32, (rows, D_DA), 1) // DA_HEAD_DIM
        qb = jnp.broadcast_to(q_ref[...], (rows, D_DA))
        qm_ref[...] = jnp.where(r == cgrp, qb, 0.0).astype(BF16)
        m_ref[...] = jnp.full(m_ref.shape, -jnp.inf, F32)
        l_ref[...] = jnp.zeros(l_ref.shape, F32)
        acc_ref[...] = jnp.zeros(acc_ref.shape, F32)

    if co_body is not None:
        co_body()

    def update(ks, vs, ntok_valid):
        qm = qm_ref[...]
        s = jnp.concatenate([_dot_nt(qm, k) for k in ks], axis=-1)
        if ntok_valid is not None:
            tcol = lax.broadcasted_iota(jnp.int32, s.shape, 1)
            s = jnp.where(tcol < ntok_valid, s, -jnp.inf)
        m_prev = m_ref[...]
        m_new = jnp.maximum(m_prev, jnp.max(s, axis=-1, keepdims=True))
        alpha = jnp.exp2(m_prev - m_new)
        p = jnp.exp2(s - m_new)
        l_ref[...] = alpha * l_ref[...] + jnp.sum(p, axis=-1, keepdims=True)
        pb = p.astype(BF16)
        ntok = s.shape[1] // len(ks)
        pv = jnp.dot(pb[:, :ntok], vs[0], preferred_element_type=F32)
        for i in range(1, len(ks)):
            pv = pv + jnp.dot(pb[:, i * ntok:(i + 1) * ntok], vs[i], preferred_element_type=F32)
        acc_ref[...] = alpha * acc_ref[...] + pv
        m_ref[...] = m_new

    def load_k(r):
        n_hc = 2 * DA_HEADS
        return jnp.concatenate(
            [r[pl.ds(hc, PAGE_SIZE, stride=n_hc), :].astype(BF16) for hc in range(n_hc)], axis=-1)

    def load_v(r):
        n_half = DA_DV // LANES
        return jnp.concatenate(
            [r[pl.ds(j * DA_HEADS + h, PAGE_SIZE, stride=n_half * DA_HEADS), :].astype(BF16)
             for h in range(DA_HEADS) for j in range(n_half)], axis=-1)

    update([load_k(r) for r in k_refs], [load_v(r) for r in v_refs], None)

    @pl.when(p_idx == pl.num_programs(step_axis) - 1)
    def _():
        r = lax.broadcasted_iota(jnp.int32, (rows, D_DA), 0)
        kn = jnp.where(r == 0, jnp.broadcast_to(kn_ref[...], (rows, D_DA)), 0.0).astype(BF16)
        vn = jnp.where(r == 0, jnp.broadcast_to(vn_ref[...], (rows, D_DA)), 0.0).astype(BF16)
        update([kn], [vn], 1)
        lam = _lam_value(lq1_ref[...], lk1_ref[...], lq2_ref[...], lk2_ref[...], lam_init)
        o = acc_ref[...] / l_ref[...]
        g = g_ref[...]
        subg = subg_ref[...]
        for h in range(DA_HEADS):
            sl = slice(h * DA_DV, (h + 1) * DA_DV)
            att = o[2 * h:2 * h + 1, sl] - lam * o[2 * h + 1:2 * h + 2, sl]
            o_ref[:, sl] = _subln_gate(att, subg, g[:, sl], lam_init)


class _DecodeJob(NamedTuple):
    page_table: jax.Array
    cache_k: jax.Array
    cache_v: jax.Array
    q: jax.Array
    k_new: jax.Array
    v_new: jax.Array
    g_da: jax.Array
    lam_vecs: list
    subg: jax.Array
    lam_init: float
    layer: int


class _DecodeOperands(NamedTuple):
    inputs: list
    in_specs: list
    out_shape: jax.ShapeDtypeStruct
    out_spec: pl.BlockSpec
    scratch_shapes: list


def _decode_operands(job, npg, first, n_seq, seq_and_step):
    def const(shape):
        return pl.BlockSpec(shape, lambda *a: (0,) * len(shape))

    row = pl.BlockSpec((None, 1, D_DA), lambda *a: (seq_and_step(a[:-1])[0], 0, 0))

    def page_spec(i):
        def index(*a):
            b, step = seq_and_step(a[:-1])
            return (job.layer * N_POOL + a[-1][b, step * npg + i], 0, 0)
        return pl.BlockSpec((None, PAGE_SIZE * 2 * DA_HEADS, LANES), index)

    pages = [page_spec(i) for i in range(npg)]
    return _DecodeOperands(
        inputs=[*job.lam_vecs, job.subg, job.q, job.k_new, job.v_new, job.g_da,
                *([job.cache_k] * npg), *([job.cache_v] * npg)],
        in_specs=[const((1, DA_HEAD_DIM))] * 4 + [const((1, DA_DV))] + [row] * 4 + pages + pages,
        out_shape=jax.ShapeDtypeStruct((n_seq, 1, D_DA), F32),
        out_spec=pl.BlockSpec((None, 1, D_DA), lambda *a: (seq_and_step(a[:-1])[0] - first, 0, 0)),
        scratch_shapes=[pltpu.VMEM((SAMPLE_ROWS, D_DA), BF16),
                        pltpu.VMEM((SAMPLE_ROWS, 1), F32), pltpu.VMEM((SAMPLE_ROWS, 1), F32),
                        pltpu.VMEM((SAMPLE_ROWS, D_DA), F32)])


def _dec_ret_kernel(st_ref, qc_ref, kc_ref, v_ref, g_ref, nst_ref, o_ref):
    v_all = v_ref[...]
    g_all = g_ref[...]
    for h in range(RET_HEADS):
        dec = float(1.0 - 2.0 ** (-5.0 - h))
        sl = slice(h * RET_DV, (h + 1) * RET_DV)
        st = st_ref[h]
        qc = qc_ref[h]
        kc = kc_ref[h]
        v = v_all[:, sl]
        nst_ref[h] = st * dec + kc * v
        cross = jnp.sum(qc * st, axis=0, keepdims=True) * dec
        intra = jnp.sum(qc * kc, axis=0, keepdims=True) * v
        o_ref[:, sl] = _rms_rows(intra + cross) * _silu(g_all[:, sl])


def _decode_retention(state, rq_col, rk_col, rv, g_ret, layer):
    col = pl.BlockSpec((None, RET_HEADS, RET_DK, 1), lambda b: (b, 0, 0, 0))
    row = pl.BlockSpec((None, 1, D_RET), lambda b: (b, 0, 0))
    return pl.pallas_call(
        _dec_ret_kernel,
        out_shape=[jax.ShapeDtypeStruct((DEC_BATCH, RET_HEADS, RET_DK, RET_DV), F32),
                   jax.ShapeDtypeStruct((DEC_BATCH, 1, D_RET), F32)],
        grid=(DEC_BATCH,),
        in_specs=[pl.BlockSpec((None, None, RET_HEADS, RET_DK, RET_DV), lambda b: (layer, b, 0, 0, 0)),
                  col, col, row, row],
        out_specs=[pl.BlockSpec((None, RET_HEADS, RET_DK, RET_DV), lambda b: (b, 0, 0, 0)),
                   pl.BlockSpec((None, 1, D_RET), lambda b: (b, 0, 0))],
        compiler_params=_cparams(1),
        name="decode_retention",
    )(state, rq_col, rk_col, rv, g_ret)


def _rotary_tables(pos):
    half = RET_DK // 2
    theta = 1.0 / (ROPE_BASE ** jnp.linspace(0.0, 1.0, half, dtype=F32))
    ang = pos.astype(F32)[:, None] * theta[None, :]
    cos, sin = jnp.cos(ang), jnp.sin(ang)
    return jnp.concatenate([cos, cos], axis=-1), jnp.concatenate([-sin, sin], axis=-1)


def _pad_rows(x):
    return jnp.pad(x, ((0, SAMPLE_ROWS - x.shape[0]), (0, 0)))


def kernel(x_prompt, x_sample, cache_k, cache_v, state_ret, page_table, norm_g, w_in, w_out,
           qn_g, kn_g, lam_q1, lam_k1, lam_q2, lam_k2, subln_g):
    t_p = BATCH * SEQ
    w_in_bf = w_in.astype(BF16)
    w_out_bf = w_out.astype(BF16)
    cos_p, sin_p = _rotary_tables(jnp.arange(SEQ))
    cos_s, sin_s = _rotary_tables(jnp.full((SAMPLE_ROWS,), PAST_LEN))
    ck = cache_k.reshape(DEPTH * N_POOL, PAGE_SIZE * DA_HEADS * 2, DA_HEAD_DIM)
    cv = _values_to_cache_order(cache_v)

    xp = x_prompt.reshape(t_p, D_MODEL)
    xs = _pad_rows(x_sample.reshape(DEC_BATCH, D_MODEL))
    hp = _input_norm(xp, norm_g[0][None], PROMPT_TM)
    hs = _input_norm(xs, norm_g[0][None], SAMPLE_ROWS)

    kp_all = vp_all = None
    sp_l, ks_l, vs_l, ss_l = [], [], [], []
    for l in range(DEPTH):
        lam_init = 0.8 - 0.6 * math.exp(-0.3 * l)
        lam_vecs = [a[l][None] for a in (lam_q1, lam_k1, lam_q2, lam_k2)]
        subg = subln_g[l][None]
        g_next = norm_g[l + 1][None] if l + 1 < DEPTH else None

        qs, k32, k_row, v32, v_row, g_da_s, rq_s, rk_s, rv_s, g_ret_s, _ = _project(
            hs, w_in_bf, l, qn_g[l][None], kn_g[l][None], cos_s, sin_s, SAMPLE_ROWS, F32)
        as_row = lambda z: z[:DEC_BATCH].reshape(DEC_BATCH, 1, z.shape[-1])
        as_col = lambda z: z[:DEC_BATCH].reshape(DEC_BATCH, RET_HEADS, RET_DK, 1)
        job = _DecodeJob(page_table, ck, cv, as_row(qs), as_row(k_row), as_row(v_row), as_row(g_da_s),
                         lam_vecs, subg, lam_init, l)

        q, kp_all, kbf, vp_all, vbf, g_da, rq, rk, rv, g_ret, dec_rows = _project(
            hp, w_in_bf, l, qn_g[l][None], kn_g[l][None], cos_p, sin_p, PROMPT_TM, BF16,
            kv_stacks=(kp_all, vp_all), decode_job=job)
        a = _prompt_attention(q, kbf, vbf, g_da, lam_vecs, subg, lam_init)
        r, st = _prompt_retention(rq, rk, rv, g_ret)
        xp, hp, dec_row = _merge(a, r, w_out_bf, l, xp, g_next, MERGE_TM, decode=(job, len(dec_rows)))
        dec_rows.append(dec_row)
        sp_l.append(st)

        assert len(dec_rows) == DEC_BATCH
        a = jnp.concatenate(dec_rows, axis=0)
        rq, rk, rv, g_ret = rq_s, rk_s, rv_s, g_ret_s
        nst, r = _decode_retention(state_ret, as_col(rq), as_col(rk), as_row(rv), as_row(g_ret), l)
        a16 = _pad_rows(a.reshape(DEC_BATCH, D_DA)).astype(BF16)
        r16 = _pad_rows(r.reshape(DEC_BATCH, D_RET)).astype(BF16)
        xs, hs, _ = _merge(a16, r16, w_out_bf, l, xs, g_next, SAMPLE_ROWS)
        ks_l.append(_keys_from_cache_order(k32, (SAMPLE_ROWS, DEC_SEQ))[:DEC_BATCH])
        vs_l.append(_values_from_cache_order(v32, (SAMPLE_ROWS, DEC_SEQ))[:DEC_BATCH])
        ss_l.append(nst)

    return (xp.reshape(BATCH, SEQ, D_MODEL),
            xs[:DEC_BATCH].reshape(DEC_BATCH, DEC_SEQ, D_MODEL),
            _keys_from_cache_order(kp_all, (DEPTH, BATCH, SEQ)),
            _values_from_cache_order(vp_all, (DEPTH, BATCH, SEQ)), jnp.stack(sp_l),
            jnp.stack(ks_l), jnp.stack(vs_l), jnp.stack(ss_l))
```

```python
import functools
import math
from typing import NamedTuple

import numpy as np
import jax
import jax.numpy as jnp
from jax import lax
from jax.experimental import pallas as pl
from jax.experimental.pallas import tpu as pltpu

D_MODEL = 2048
BATCH = 4
SEQ = 2048
DEPTH = 4
DEC_BATCH = 8
DEC_SEQ = 1
PAST_LEN = 16384
PAGE_SIZE = 128
N_PAGES = PAST_LEN // PAGE_SIZE
N_POOL = (DEC_BATCH * N_PAGES * 5) // 4

D_DA = D_MODEL // 2
D_RET = D_MODEL - D_DA
DA_HEAD_DIM = 128
DA_HEADS = D_DA // (2 * DA_HEAD_DIM)
DA_DV = 2 * DA_HEAD_DIM
RET_HEADS = 4
RET_DV = D_RET // RET_HEADS
RET_DK = RET_DV // 2
ROPE_BASE = 10000.0
EPS = 1e-6
D_IN = 4 * D_DA + 2 * RET_HEADS * RET_DK + 2 * D_RET

COL_Q, COL_K, COL_V, COL_GDA = 0, D_DA, 2 * D_DA, 3 * D_DA
COL_RQ = 4 * D_DA
COL_RK = COL_RQ + RET_HEADS * RET_DK
COL_RV = COL_RK + RET_HEADS * RET_DK
COL_GRET = COL_RV + D_RET

LANES = 128
SAMPLE_ROWS = 16
VMEM_LIMIT = 48 * 1024 * 1024

PROMPT_TM = 512
MERGE_TM = 256
N_COL_CHUNKS = 4
DEC_PAGE_GROUPS = 2
ATTN_TQ = 512
Q_SCALE = DA_HEAD_DIM ** -0.5 * math.log2(math.e)
RET_CHUNK = 128
RET_TILE = 512

LOG_DECAY = np.log(1.0 - 2.0 ** (-5.0 - np.arange(RET_HEADS, dtype=np.float32))).astype(np.float32)

F32 = jnp.float32
BF16 = jnp.bfloat16


def _cparams(n_axes):
    return pltpu.CompilerParams(dimension_semantics=("arbitrary",) * n_axes,
                                vmem_limit_bytes=VMEM_LIMIT)


def _silu(g):
    return g / (1.0 + jnp.exp(-g))


def _rms_rows(x):
    return x * lax.rsqrt(jnp.mean(x * x, axis=-1, keepdims=True) + EPS)


def _dot_nt(a, b):
    return lax.dot_general(a, b, (((1,), (1,)), ((), ())), preferred_element_type=F32)


def _dot_tn(a, b):
    return lax.dot_general(a, b, (((0,), (0,)), ((), ())), preferred_element_type=F32)


def _norm_kernel(x_ref, g_ref, h_ref):
    h_ref[...] = (_rms_rows(x_ref[...]) * g_ref[...]).astype(BF16)


def _input_norm(x, g, tm):
    t = x.shape[0]
    return pl.pallas_call(
        _norm_kernel,
        out_shape=jax.ShapeDtypeStruct((t, D_MODEL), BF16),
        grid=(t // tm,),
        in_specs=[pl.BlockSpec((tm, D_MODEL), lambda m: (m, 0)),
                  pl.BlockSpec((1, D_MODEL), lambda m: (0, 0))],
        out_specs=pl.BlockSpec((tm, D_MODEL), lambda m: (m, 0)),
        compiler_params=_cparams(1),
        name="input_norm",
    )(x, g)


def _chunk_dots(h_ref, w_ref, n_chunks, epilogue):
    width = w_ref.shape[1] // n_chunks

    def part(c):
        z = jnp.dot(h_ref[...], w_ref[:, c * width:(c + 1) * width], preferred_element_type=F32)
        epilogue(c, z)

    return [functools.partial(part, c) for c in range(n_chunks)]


def _proj_qk_parts(h_ref, w_ref, g_ref, *out_refs, scale):
    n_grp = D_DA // DA_HEAD_DIM
    grp_per_chunk = n_grp // N_COL_CHUNKS

    def epilogue(c, z):
        tm = z.shape[0]
        g = g_ref[...]
        for jj in range(grp_per_chunk):
            j = c * grp_per_chunk + jj
            y = _rms_rows(z[:, jj * DA_HEAD_DIM:(jj + 1) * DA_HEAD_DIM]) * g
            if scale is not None:
                y = y * scale
            for o in out_refs:
                if o.shape[1] == DA_HEAD_DIM:
                    o[pl.ds(j, tm, stride=n_grp), :] = y.astype(o.dtype)
                else:
                    o[:, j * DA_HEAD_DIM:(j + 1) * DA_HEAD_DIM] = y.astype(o.dtype)

    return _chunk_dots(h_ref, w_ref, N_COL_CHUNKS, epilogue)


def _proj_v_parts(h_ref, w_ref, v_ref, vact_ref):
    assert N_COL_CHUNKS == DA_HEADS

    def epilogue(h, z):
        tm = z.shape[0]
        for j in range(DA_DV // LANES):
            v_ref[pl.ds(j * DA_HEADS + h, tm, stride=2 * DA_HEADS), :] = z[:, j * LANES:(j + 1) * LANES]
        vact_ref[:, h * DA_DV:(h + 1) * DA_DV] = z.astype(vact_ref.dtype)

    return _chunk_dots(h_ref, w_ref, N_COL_CHUNKS, epilogue)


def _proj_copy_parts(h_ref, w_ref, *out_refs):
    def epilogue(c, z):
        width = z.shape[1]
        for o in out_refs:
            o[:, c * width:(c + 1) * width] = z.astype(o.dtype)

    return _chunk_dots(h_ref, w_ref, N_COL_CHUNKS, epilogue)


def _proj_gate_parts(h_ref, w_ref, o_ref):
    def epilogue(c, z):
        width = z.shape[1]
        o_ref[:, c * width:(c + 1) * width] = _silu(z).astype(o_ref.dtype)

    return _chunk_dots(h_ref, w_ref, N_COL_CHUNKS, epilogue)


def _proj_rot_parts(h_ref, w_ref, cos_ref, sin_ref, rq_ref, rk_ref):
    grp_per_chunk = 2 * RET_HEADS // N_COL_CHUNKS

    def epilogue(c, z):
        cos2 = cos_ref[...]
        sin2 = sin_ref[...]
        for jj in range(grp_per_chunk):
            j = c * grp_per_chunk + jj
            x = z[:, jj * RET_DK:(jj + 1) * RET_DK]
            if j >= RET_HEADS:
                x = x * (RET_DK ** -0.5)
            y = x * cos2 + pltpu.roll(x, RET_DK // 2, axis=1) * sin2
            if j < RET_HEADS:
                rq_ref[:, j * RET_DK:(j + 1) * RET_DK] = y.astype(rq_ref.dtype)
            else:
                jk = j - RET_HEADS
                rk_ref[:, jk * RET_DK:(jk + 1) * RET_DK] = y.astype(rk_ref.dtype)

    return _chunk_dots(h_ref, w_ref, N_COL_CHUNKS, epilogue)


def _run_parts(parts_fn):
    def kern(*refs):
        for part in parts_fn(*refs):
            part()
    return kern


def _with_carried_buffer(parts_fn):
    def wrapped(carried_ref, *refs):
        del carried_ref
        return parts_fn(*refs)
    return wrapped


def _with_decode(parts_fn, dec_kern, n_tile_in, n_dec_in, n_tile_out):
    def wrapped(pt_ref, *refs):
        tile_in = refs[:n_tile_in]
        dec_in = refs[n_tile_in:n_tile_in + n_dec_in]
        outs = refs[n_tile_in + n_dec_in:]
        dec_kern(pt_ref, *dec_in, *outs[n_tile_out:], co_parts=parts_fn(*tile_in, *outs[:n_tile_out]))
    return wrapped


def _proj_call(kern, h, w_in, layer, col0, ncols, tm, extra, extra_specs, outs, name, stacked=None,
               decode=None):
    t = h.shape[0]
    nm = t // tm
    assert col0 % ncols == 0
    cb = col0 // ncols
    out_shape = [jax.ShapeDtypeStruct((t * r, w), dt) for (r, w, dt) in outs]
    out_specs = [pl.BlockSpec((tm * r, w), lambda m, *_: (m, 0)) for (r, w, _) in outs]
    inputs = [h, w_in, *extra]
    in_specs = [pl.BlockSpec((tm, D_MODEL), lambda m, *_: (m, 0)),
                pl.BlockSpec((None, D_MODEL, ncols), lambda m, *_: (layer, 0, cb))] + extra_specs
    aliases = {}
    if stacked is not None:
        i, buf = stacked
        r, w, dt = outs[i]
        out_shape[i] = jax.ShapeDtypeStruct((DEPTH * t * r, w), dt)
        out_specs[i] = pl.BlockSpec((tm * r, w), lambda m, *_: (layer * nm + m, 0))
        if buf is not None:
            kern = _with_carried_buffer(kern)
            inputs = [buf] + inputs
            in_specs = [pl.BlockSpec(memory_space=pl.ANY)] + in_specs
            aliases = {0: i}
    return _launch_rows(kern, nm, inputs, in_specs, out_shape, out_specs, aliases, name, decode)


def _launch_rows(kern, nm, inputs, in_specs, out_shape, out_specs, aliases, name, decode):
    if decode is None:
        return pl.pallas_call(
            _run_parts(kern),
            out_shape=out_shape,
            grid=(nm,),
            in_specs=in_specs,
            out_specs=out_specs,
            input_output_aliases=aliases,
            compiler_params=_cparams(1),
            name=name,
        )(*inputs)

    job, b = decode
    npg = N_PAGES // nm
    assert npg * nm == N_PAGES
    dec = _decode_operands(job, npg, b, 1, lambda idx: (b, idx[0]))
    body = _with_decode(kern, functools.partial(_dec_attn_kernel, lam_init=job.lam_init, npg=npg, step_axis=0),
                        len(inputs), len(dec.inputs), len(out_shape))
    grid_spec = pltpu.PrefetchScalarGridSpec(
        num_scalar_prefetch=1,
        grid=(nm,),
        in_specs=in_specs + dec.in_specs,
        out_specs=out_specs + [dec.out_spec],
        scratch_shapes=dec.scratch_shapes,
    )
    return pl.pallas_call(
        body,
        out_shape=out_shape + [dec.out_shape],
        grid_spec=grid_spec,
        input_output_aliases={k + 1: v for k, v in aliases.items()},
        compiler_params=_cparams(1),
        name=name + "_dec",
    )(job.page_table, *inputs, *dec.inputs)


N_PROJ_CALLS = 7


def _project(h, w_in, layer, qn_g, kn_g, cos2, sin2, tm, act_dtype, kv_stacks=None, decode_job=None):
    k_stacked = None if kv_stacks is None else (0, kv_stacks[0])
    v_stacked = None if kv_stacks is None else (0, kv_stacks[1])
    gspec = [pl.BlockSpec((1, DA_HEAD_DIM), lambda m, *_: (0, 0))]
    cache_rows = 2 * DA_HEADS
    dec_rows = []

    def call(kern, col0, ncols, extra, extra_specs, outs, name, stacked=None):
        decode = None if decode_job is None else (decode_job, len(dec_rows))
        res = _proj_call(kern, h, w_in, layer, col0, ncols, tm, extra, extra_specs, outs, name,
                         stacked=stacked, decode=decode)
        if decode is not None:
            dec_rows.append(res[-1])
            res = res[:-1]
        return res

    (q,) = call(functools.partial(_proj_qk_parts, scale=Q_SCALE), COL_Q, D_DA, [qn_g], gspec,
                [(1, D_DA, act_dtype)], "proj_q")
    k32, kact = call(functools.partial(_proj_qk_parts, scale=None), COL_K, D_DA, [kn_g], gspec,
                     [(cache_rows, DA_HEAD_DIM, F32), (1, D_DA, act_dtype)], "proj_k", stacked=k_stacked)
    v32, vact = call(_proj_v_parts, COL_V, D_DA, [], [],
                     [(cache_rows, LANES, F32), (1, D_DA, act_dtype)], "proj_v", stacked=v_stacked)
    (g_da,) = call(_proj_gate_parts, COL_GDA, D_DA, [], [], [(1, D_DA, act_dtype)], "proj_gda")
    n_tab = cos2.shape[0] // tm
    tspec = [pl.BlockSpec((tm, RET_DK), lambda m, *_: (m % n_tab, 0))] * 2
    nrot = 2 * RET_HEADS * RET_DK
    rq, rk = call(_proj_rot_parts, COL_RQ, nrot, [cos2, sin2], tspec,
                  [(1, nrot // 2, act_dtype), (1, nrot // 2, act_dtype)], "proj_rot")
    (rv,) = call(_proj_copy_parts, COL_RV, D_RET, [], [], [(1, D_RET, act_dtype)], "proj_rv")
    (g_ret,) = call(_proj_gate_parts, COL_GRET, D_RET, [], [], [(1, D_RET, act_dtype)], "proj_gret")
    assert decode_job is None or len(dec_rows) == N_PROJ_CALLS
    return q, k32, kact, v32, vact, g_da, rq, rk, rv, g_ret, dec_rows


def _keys_from_cache_order(k32, lead):
    return k32.reshape(lead + (DA_HEADS, 2, DA_HEAD_DIM))


def _values_from_cache_order(v32, lead):
    n_half = DA_DV // LANES
    v = v32.reshape(lead + (n_half, DA_HEADS, LANES))
    nd = len(lead)
    v = v.transpose(tuple(range(nd)) + (nd + 1, nd, nd + 2))
    return v.reshape(lead + (DA_HEADS, DA_DV))


def _values_to_cache_order(cache_v):
    n_half = DA_DV // LANES
    v = cache_v.reshape(-1, PAGE_SIZE, DA_HEADS, n_half, LANES).transpose(0, 1, 3, 2, 4)
    return v.reshape(-1, PAGE_SIZE * n_half * DA_HEADS, LANES)


def _lam_value(lq1, lk1, lq2, lk2, lam_init):
    s1 = jnp.sum(lq1 * lk1, axis=-1, keepdims=True)
    s2 = jnp.sum(lq2 * lk2, axis=-1, keepdims=True)
    return jnp.exp(s1) - jnp.exp(s2) + lam_init


def _subln_gate(o, subg, gate, lam_init):
    return (_rms_rows(o) * subg) * (1.0 - lam_init) * gate.astype(F32)


def _attn_kernel(lq1_ref, lk1_ref, lq2_ref, lk2_ref, subg_ref, q_ref, k_ref, v_ref, g_ref, o_ref,
                 m_ref, l_ref, acc_ref, *, lam_init, tq):
    qi = pl.program_id(2)
    q = q_ref[...]
    d = DA_HEAD_DIM

    m_ref[...] = jnp.full(m_ref.shape, -jnp.inf, F32)
    l_ref[...] = jnp.zeros(l_ref.shape, F32)
    acc_ref[...] = jnp.zeros(acc_ref.shape, F32)

    def step(r0, nr, k, v, mask):
        rows = slice(r0, r0 + nr)
        n_lt = k.shape[0] // LANES
        upd = []
        for c in range(2):
            s = _dot_nt(q[rows, c * d:(c + 1) * d], k[:, c * d:(c + 1) * d])
            if mask is not None:
                s = jnp.where(mask, s, -jnp.inf)
            st = [s[:, j * LANES:(j + 1) * LANES] for j in range(n_lt)]
            fold = st[0]
            for j in range(1, n_lt):
                fold = jnp.maximum(fold, st[j])
            m_prev = m_ref[c, rows]
            m_new = jnp.maximum(m_prev, jnp.broadcast_to(jnp.max(fold, axis=-1, keepdims=True),
                                                         (nr, LANES)))
            alpha = jnp.exp2(m_prev - m_new)
            pt = [jnp.exp2(t - m_new) for t in st]
            psum = pt[0]
            for j in range(1, n_lt):
                psum = psum + pt[j]
            p = jnp.concatenate([t.astype(BF16) for t in pt], axis=-1)
            pv = jnp.dot(p, v, preferred_element_type=F32)
            upd.append((m_new, alpha, psum, pv))
        for c, (m_new, alpha, psum, pv) in enumerate(upd):
            m_ref[c, rows] = m_new
            l_ref[c, rows] = alpha * l_ref[c, rows] + psum
            acc_ref[c, rows] = jnp.concatenate([alpha] * (DA_DV // LANES), axis=-1) * acc_ref[c, rows] + pv

    def body(ki, carry):
        off = pl.multiple_of(ki * tq, tq)
        step(0, tq, k_ref[pl.ds(off, tq), :], v_ref[pl.ds(off, tq), :], None)
        return carry

    lax.fori_loop(0, qi, body, 0)

    off = pl.multiple_of(qi * tq, tq)
    hq = tq // 2
    for r0, nk in ((0, hq), (hq, tq)):
        row = lax.broadcasted_iota(jnp.int32, (hq, nk), 0) + r0
        col = lax.broadcasted_iota(jnp.int32, (hq, nk), 1)
        step(r0, hq, k_ref[pl.ds(off, nk), :], v_ref[pl.ds(off, nk), :], col <= row)

    lam = _lam_value(lq1_ref[...], lk1_ref[...], lq2_ref[...], lk2_ref[...], lam_init)
    l1 = jnp.sum(l_ref[0], axis=-1, keepdims=True)
    l2 = jnp.sum(l_ref[1], axis=-1, keepdims=True)
    o = acc_ref[0] / l1 - lam * (acc_ref[1] / l2)
    o_ref[...] = _subln_gate(o, subg_ref[...], g_ref[...], lam_init).astype(o_ref.dtype)


def _prompt_attention(q, k, v, g_da, lam_vecs, subg, lam_init):
    tq = ATTN_TQ
    nq = SEQ // tq
    vec = pl.BlockSpec((1, DA_HEAD_DIM), lambda b, h, i: (0, 0))
    return pl.pallas_call(
        functools.partial(_attn_kernel, lam_init=lam_init, tq=tq),
        out_shape=jax.ShapeDtypeStruct((BATCH * SEQ, D_DA), BF16),
        grid=(BATCH, DA_HEADS, nq),
        in_specs=[vec, vec, vec, vec,
                  pl.BlockSpec((1, DA_DV), lambda b, h, i: (0, 0)),
                  pl.BlockSpec((tq, DA_DV), lambda b, h, i: (b * nq + i, h)),
                  pl.BlockSpec((SEQ, DA_DV), lambda b, h, i: (b, h)),
                  pl.BlockSpec((SEQ, DA_DV), lambda b, h, i: (b, h)),
                  pl.BlockSpec((tq, DA_DV), lambda b, h, i: (b * nq + i, h))],
        out_specs=pl.BlockSpec((tq, DA_DV), lambda b, h, i: (b * nq + i, h)),
        scratch_shapes=[pltpu.VMEM((2, tq, LANES), F32), pltpu.VMEM((2, tq, LANES), F32),
                        pltpu.VMEM((2, tq, DA_DV), F32)],
        compiler_params=_cparams(3),
        name="prompt_attention",
    )(*lam_vecs, subg, q, k, v, g_da)


def _ret_kernel(q_ref, k_ref, v_ref, g_ref, o_ref, st_ref, *, chunk):
    c = chunk
    n_chunks = q_ref.shape[0] // c

    @pl.when(pl.program_id(1) == 0)
    def _():
        st_ref[...] = jnp.zeros(st_ref.shape, F32)

    li = lax.broadcasted_iota(jnp.int32, (c, 1), 0).astype(F32)
    diff = (lax.broadcasted_iota(jnp.int32, (c, c), 0)
            - lax.broadcasted_iota(jnp.int32, (c, c), 1)).astype(F32)
    decays = []
    for h in range(RET_HEADS):
        lg = float(LOG_DECAY[h])
        decays.append((jnp.where(diff >= 0, jnp.exp(lg * jnp.maximum(diff, 0.0)), 0.0),
                       jnp.exp((li + 1.0) * lg), jnp.exp((c - 1.0 - li) * lg), math.exp(c * lg)))

    def body(i, carry):
        off = pl.multiple_of(i * c, c)
        for h in range(RET_HEADS):
            dmask, cross_dec, k_dec, st_dec = decays[h]
            q = q_ref[pl.ds(off, c), h * RET_DK:(h + 1) * RET_DK]
            k = k_ref[pl.ds(off, c), h * RET_DK:(h + 1) * RET_DK]
            v = v_ref[pl.ds(off, c), h * RET_DV:(h + 1) * RET_DV]
            state = st_ref[h]
            s = _dot_nt(q, k) * dmask
            intra = jnp.dot(s.astype(BF16), v, preferred_element_type=F32)
            cross = jnp.dot(q, state.astype(BF16), preferred_element_type=F32) * cross_dec
            kd = (k.astype(F32) * k_dec).astype(BF16)
            st_ref[h] = state * st_dec + _dot_tn(kd, v)
            gate = g_ref[pl.ds(off, c), h * RET_DV:(h + 1) * RET_DV].astype(F32)
            o_ref[pl.ds(off, c), h * RET_DV:(h + 1) * RET_DV] = (
                _rms_rows(intra + cross) * gate).astype(o_ref.dtype)
        return carry

    lax.fori_loop(0, n_chunks, body, 0)


def _prompt_retention(rq, rk, rv, g_ret):
    ts = RET_TILE
    ns = SEQ // ts
    nqk = RET_HEADS * RET_DK
    return pl.pallas_call(
        functools.partial(_ret_kernel, chunk=RET_CHUNK),
        out_shape=[jax.ShapeDtypeStruct((BATCH * SEQ, D_RET), BF16),
                   jax.ShapeDtypeStruct((BATCH, RET_HEADS, RET_DK, RET_DV), F32)],
        grid=(BATCH, ns),
        in_specs=[pl.BlockSpec((ts, nqk), lambda b, i: (b * ns + i, 0)),
                  pl.BlockSpec((ts, nqk), lambda b, i: (b * ns + i, 0)),
                  pl.BlockSpec((ts, D_RET), lambda b, i: (b * ns + i, 0)),
                  pl.BlockSpec((ts, D_RET), lambda b, i: (b * ns + i, 0))],
        out_specs=[pl.BlockSpec((ts, D_RET), lambda b, i: (b * ns + i, 0)),
                   pl.BlockSpec((None, RET_HEADS, RET_DK, RET_DV), lambda b, i: (b, 0, 0, 0))],
        compiler_params=_cparams(2),
        name="prompt_retention",
    )(rq, rk, rv, g_ret)


def _merge_parts(a_ref, r_ref, w_ref, x_ref, g_ref, y_ref, *h_ref):
    width = D_MODEL // N_COL_CHUNKS

    def part(c):
        cols = slice(c * width, (c + 1) * width)
        ar = jnp.concatenate([a_ref[...], r_ref[...]], axis=-1)
        y_ref[:, cols] = x_ref[:, cols] + jnp.dot(ar, w_ref[:, cols], preferred_element_type=F32)

    def norm_part():
        h_ref[0][...] = (_rms_rows(y_ref[...]) * g_ref[...]).astype(BF16)

    parts = [functools.partial(part, c) for c in range(N_COL_CHUNKS)]
    return parts + [norm_part] if h_ref else parts


def _merge(a, r, w_out, layer, x, g_next, tm, decode=None):
    t = x.shape[0]
    with_h = g_next is not None
    out_shape = [jax.ShapeDtypeStruct((t, D_MODEL), F32)]
    out_specs = [pl.BlockSpec((tm, D_MODEL), lambda m, *_: (m, 0))]
    if with_h:
        out_shape.append(jax.ShapeDtypeStruct((t, D_MODEL), BF16))
        out_specs.append(pl.BlockSpec((tm, D_MODEL), lambda m, *_: (m, 0)))
    else:
        g_next = jnp.ones((1, D_MODEL), F32)
    in_specs = [pl.BlockSpec((tm, D_DA), lambda m, *_: (m, 0)),
                pl.BlockSpec((tm, D_RET), lambda m, *_: (m, 0)),
                pl.BlockSpec((None, D_MODEL, D_MODEL), lambda m, *_: (layer, 0, 0)),
                pl.BlockSpec((tm, D_MODEL), lambda m, *_: (m, 0)),
                pl.BlockSpec((1, D_MODEL), lambda m, *_: (0, 0))]
    res = list(_launch_rows(_merge_parts, t // tm, [a, r, w_out, x, g_next], in_specs, out_shape, out_specs,
                            {}, "merge", decode))
    dec_row = res.pop() if decode is not None else None
    return res[0], (res[1] if with_h else None), dec_row


def _dec_attn_kernel(pt_ref, lq1_ref, lk1_ref, lq2_ref, lk2_ref, subg_ref,
                     q_ref, kn_ref, vn_ref, g_ref, *rest, lam_init, npg, step_axis, co_parts):
    del pt_ref
    k_refs = rest[:npg]
    v_refs = rest[npg:2 * npg]
    o_ref = rest[2 * npg]
    qm_ref, m_ref, l_ref, acc_ref = rest[2 * npg + 1:]
    p_idx = pl.program_id(step_axis)
    rows = SAMPLE_ROWS

    @pl.when(p_idx == 0)
    def _():
        r = lax.broadcasted_iota(jnp.int32, (rows, D_DA), 0)
        cgrp = lax.broadcasted_iota(jnp.int32, (rows, D_DA), 1) // DA_HEAD_DIM
        qb = jnp.broadcast_to(q_ref[...], (rows, D_DA))
        qm_ref[...] = jnp.where(r == cgrp, qb, 0.0).astype(BF16)
        m_ref[...] = jnp.full(m_ref.shape, -jnp.inf, F32)
        l_ref[...] = jnp.zeros(l_ref.shape, F32)
        acc_ref[...] = jnp.zeros(acc_ref.shape, F32)

    def score_part(ks, ntok_valid):
        qm = qm_ref[...]
        s = jnp.concatenate([_dot_nt(qm, k) for k in ks], axis=-1)
        if ntok_valid is not None:
            tcol = lax.broadcasted_iota(jnp.int32, s.shape, 1)
            s = jnp.where(tcol < ntok_valid, s, -jnp.inf)
        m_prev = m_ref[...]
        m_new = jnp.maximum(m_prev, jnp.max(s, axis=-1, keepdims=True))
        alpha = jnp.exp2(m_prev - m_new)
        p = jnp.exp2(s - m_new)
        l_ref[...] = alpha * l_ref[...] + jnp.sum(p, axis=-1, keepdims=True)
        m_ref[...] = m_new
        return alpha, p.astype(BF16)

    def value_part(alpha, pb, vs):
        ntok = pb.shape[1] // len(vs)
        pv = jnp.dot(pb[:, :ntok], vs[0], preferred_element_type=F32)
        for i in range(1, len(vs)):
            pv = pv + jnp.dot(pb[:, i * ntok:(i + 1) * ntok], vs[i], preferred_element_type=F32)
        acc_ref[...] = alpha * acc_ref[...] + pv

    def load_k(r):
        n_hc = 2 * DA_HEADS
        return jnp.concatenate(
            [r[pl.ds(hc, PAGE_SIZE, stride=n_hc), :].astype(BF16) for hc in range(n_hc)], axis=-1)

    def load_v(r):
        n_half = DA_DV // LANES
        return jnp.concatenate(
            [r[pl.ds(j * DA_HEADS + h, PAGE_SIZE, stride=n_half * DA_HEADS), :].astype(BF16)
             for h in range(DA_HEADS) for j in range(n_half)], axis=-1)

    assert npg % DEC_PAGE_GROUPS == 0
    per = npg // DEC_PAGE_GROUPS
    carried = {}
    dec_parts = []
    for gi in range(DEC_PAGE_GROUPS):
        grp = slice(gi * per, (gi + 1) * per)

        def score(grp=grp):
            carried["alpha"], carried["p"] = score_part([load_k(r) for r in k_refs[grp]], None)

        def value(grp=grp):
            value_part(carried["alpha"], carried["p"], [load_v(r) for r in v_refs[grp]])

        dec_parts += [score, value]
    co_parts = list(co_parts)
    for i in range(max(len(co_parts), len(dec_parts))):
        if i < len(co_parts):
            co_parts[i]()
        if i < len(dec_parts):
            dec_parts[i]()

    @pl.when(p_idx == pl.num_programs(step_axis) - 1)
    def _():
        r = lax.broadcasted_iota(jnp.int32, (rows, D_DA), 0)
        kn = jnp.where(r == 0, jnp.broadcast_to(kn_ref[...], (rows, D_DA)), 0.0).astype(BF16)
        vn = jnp.where(r == 0, jnp.broadcast_to(vn_ref[...], (rows, D_DA)), 0.0).astype(BF16)
        value_part(*score_part([kn], 1), [vn])
        lam = _lam_value(lq1_ref[...], lk1_ref[...], lq2_ref[...], lk2_ref[...], lam_init)
        o = acc_ref[...] / l_ref[...]
        g = g_ref[...]
        subg = subg_ref[...]
        for h in range(DA_HEADS):
            sl = slice(h * DA_DV, (h + 1) * DA_DV)
            att = o[2 * h:2 * h + 1, sl] - lam * o[2 * h + 1:2 * h + 2, sl]
            o_ref[:, sl] = _subln_gate(att, subg, g[:, sl], lam_init)


class _DecodeJob(NamedTuple):
    page_table: jax.Array
    cache_k: jax.Array
    cache_v: jax.Array
    q: jax.Array
    k_new: jax.Array
    v_new: jax.Array
    g_da: jax.Array
    lam_vecs: list
    subg: jax.Array
    lam_init: float
    layer: int


class _DecodeOperands(NamedTuple):
    inputs: list
    in_specs: list
    out_shape: jax.ShapeDtypeStruct
    out_spec: pl.BlockSpec
    scratch_shapes: list


def _decode_operands(job, npg, first, n_seq, seq_and_step):
    def const(shape):
        return pl.BlockSpec(shape, lambda *a: (0,) * len(shape))

    row = pl.BlockSpec((None, 1, D_DA), lambda *a: (seq_and_step(a[:-1])[0], 0, 0))

    def page_spec(i):
        def index(*a):
            b, step = seq_and_step(a[:-1])
            return (job.layer * N_POOL + a[-1][b, step * npg + i], 0, 0)
        return pl.BlockSpec((None, PAGE_SIZE * 2 * DA_HEADS, LANES), index)

    pages = [page_spec(i) for i in range(npg)]
    return _DecodeOperands(
        inputs=[*job.lam_vecs, job.subg, job.q, job.k_new, job.v_new, job.g_da,
                *([job.cache_k] * npg), *([job.cache_v] * npg)],
        in_specs=[const((1, DA_HEAD_DIM))] * 4 + [const((1, DA_DV))] + [row] * 4 + pages + pages,
        out_shape=jax.ShapeDtypeStruct((n_seq, 1, D_DA), F32),
        out_spec=pl.BlockSpec((None, 1, D_DA), lambda *a: (seq_and_step(a[:-1])[0] - first, 0, 0)),
        scratch_shapes=[pltpu.VMEM((SAMPLE_ROWS, D_DA), BF16),
                        pltpu.VMEM((SAMPLE_ROWS, 1), F32), pltpu.VMEM((SAMPLE_ROWS, 1), F32),
                        pltpu.VMEM((SAMPLE_ROWS, D_DA), F32)])


def _dec_ret_kernel(st_ref, qc_ref, kc_ref, v_ref, g_ref, nst_ref, o_ref):
    v_all = v_ref[...]
    g_all = g_ref[...]
    for h in range(RET_HEADS):
        dec = float(1.0 - 2.0 ** (-5.0 - h))
        sl = slice(h * RET_DV, (h + 1) * RET_DV)
        st = st_ref[h]
        qc = qc_ref[h]
        kc = kc_ref[h]
        v = v_all[:, sl]
        nst_ref[h] = st * dec + kc * v
        cross = jnp.sum(qc * st, axis=0, keepdims=True) * dec
        intra = jnp.sum(qc * kc, axis=0, keepdims=True) * v
        o_ref[:, sl] = _rms_rows(intra + cross) * g_all[:, sl]


def _decode_retention(state, rq_col, rk_col, rv, g_ret, layer):
    col = pl.BlockSpec((None, RET_HEADS, RET_DK, 1), lambda b: (b, 0, 0, 0))
    row = pl.BlockSpec((None, 1, D_RET), lambda b: (b, 0, 0))
    return pl.pallas_call(
        _dec_ret_kernel,
        out_shape=[jax.ShapeDtypeStruct((DEC_BATCH, RET_HEADS, RET_DK, RET_DV), F32),
                   jax.ShapeDtypeStruct((DEC_BATCH, 1, D_RET), F32)],
        grid=(DEC_BATCH,),
        in_specs=[pl.BlockSpec((None, None, RET_HEADS, RET_DK, RET_DV), lambda b: (layer, b, 0, 0, 0)),
                  col, col, row, row],
        out_specs=[pl.BlockSpec((None, RET_HEADS, RET_DK, RET_DV), lambda b: (b, 0, 0, 0)),
                   pl.BlockSpec((None, 1, D_RET), lambda b: (b, 0, 0))],
        compiler_params=_cparams(1),
        name="decode_retention",
    )(state, rq_col, rk_col, rv, g_ret)


def _rotary_tables(pos):
    half = RET_DK // 2
    theta = 1.0 / (ROPE_BASE ** jnp.linspace(0.0, 1.0, half, dtype=F32))
    ang = pos.astype(F32)[:, None] * theta[None, :]
    cos, sin = jnp.cos(ang), jnp.sin(ang)
    return jnp.concatenate([cos, cos], axis=-1), jnp.concatenate([-sin, sin], axis=-1)


def _pad_rows(x):
    return jnp.pad(x, ((0, SAMPLE_ROWS - x.shape[0]), (0, 0)))


def kernel(x_prompt, x_sample, cache_k, cache_v, state_ret, page_table, norm_g, w_in, w_out,
           qn_g, kn_g, lam_q1, lam_k1, lam_q2, lam_k2, subln_g):
    t_p = BATCH * SEQ
    w_in_bf = w_in.astype(BF16)
    w_out_bf = w_out.astype(BF16)
    cos_p, sin_p = _rotary_tables(jnp.arange(SEQ))
    cos_s, sin_s = _rotary_tables(jnp.full((SAMPLE_ROWS,), PAST_LEN))
    ck = cache_k.reshape(DEPTH * N_POOL, PAGE_SIZE * DA_HEADS * 2, DA_HEAD_DIM)
    cv = _values_to_cache_order(cache_v)

    xp = x_prompt.reshape(t_p, D_MODEL)
    xs = _pad_rows(x_sample.reshape(DEC_BATCH, D_MODEL))
    hp = _input_norm(xp, norm_g[0][None], PROMPT_TM)
    hs = _input_norm(xs, norm_g[0][None], SAMPLE_ROWS)

    kp_all = vp_all = None
    sp_l, ks_l, vs_l, ss_l = [], [], [], []
    for l in range(DEPTH):
        lam_init = 0.8 - 0.6 * math.exp(-0.3 * l)
        lam_vecs = [a[l][None] for a in (lam_q1, lam_k1, lam_q2, lam_k2)]
        subg = subln_g[l][None]
        g_next = norm_g[l + 1][None] if l + 1 < DEPTH else None

        qs, k32, k_row, v32, v_row, g_da_s, rq_s, rk_s, rv_s, g_ret_s, _ = _project(
            hs, w_in_bf, l, qn_g[l][None], kn_g[l][None], cos_s, sin_s, SAMPLE_ROWS, F32)
        as_row = lambda z: z[:DEC_BATCH].reshape(DEC_BATCH, 1, z.shape[-1])
        as_col = lambda z: z[:DEC_BATCH].reshape(DEC_BATCH, RET_HEADS, RET_DK, 1)
        job = _DecodeJob(page_table, ck, cv, as_row(qs), as_row(k_row), as_row(v_row), as_row(g_da_s),
                         lam_vecs, subg, lam_init, l)

        q, kp_all, kbf, vp_all, vbf, g_da, rq, rk, rv, g_ret, dec_rows = _project(
            hp, w_in_bf, l, qn_g[l][None], kn_g[l][None], cos_p, sin_p, PROMPT_TM, BF16,
            kv_stacks=(kp_all, vp_all), decode_job=job)
        a = _prompt_attention(q, kbf, vbf, g_da, lam_vecs, subg, lam_init)
        r, st = _prompt_retention(rq, rk, rv, g_ret)
        xp, hp, dec_row = _merge(a, r, w_out_bf, l, xp, g_next, MERGE_TM, decode=(job, len(dec_rows)))
        dec_rows.append(dec_row)
        sp_l.append(st)

        assert len(dec_rows) == DEC_BATCH
        a = jnp.concatenate(dec_rows, axis=0)
        rq, rk, rv, g_ret = rq_s, rk_s, rv_s, g_ret_s
        nst, r = _decode_retention(state_ret, as_col(rq), as_col(rk), as_row(rv), as_row(g_ret), l)
        a16 = _pad_rows(a.reshape(DEC_BATCH, D_DA)).astype(BF16)
        r16 = _pad_rows(r.reshape(DEC_BATCH, D_RET)).astype(BF16)
        xs, hs, _ = _merge(a16, r16, w_out_bf, l, xs, g_next, SAMPLE_ROWS)
        ks_l.append(_keys_from_cache_order(k32, (SAMPLE_ROWS, DEC_SEQ))[:DEC_BATCH])
        vs_l.append(_values_from_cache_order(v32, (SAMPLE_ROWS, DEC_SEQ))[:DEC_BATCH])
        ss_l.append(nst)

    return (xp.reshape(BATCH, SEQ, D_MODEL),
            xs[:DEC_BATCH].reshape(DEC_BATCH, DEC_SEQ, D_MODEL),
            _keys_from_cache_order(kp_all, (DEPTH, BATCH, SEQ)),
            _values_from_cache_order(vp_all, (DEPTH, BATCH, SEQ)), jnp.stack(sp_l),
            jnp.stack(ks_l), jnp.stack(vs_l), jnp.stack(ss_l))
```

```python
import functools
import math
from typing import NamedTuple

import numpy as np
import jax
import jax.numpy as jnp
from jax import lax
from jax.experimental import pallas as pl
from jax.experimental.pallas import tpu as pltpu

D_MODEL = 2048
BATCH = 4
SEQ = 2048
DEPTH = 4
DEC_BATCH = 8
DEC_SEQ = 1
PAST_LEN = 16384
PAGE_SIZE = 128
N_PAGES = PAST_LEN // PAGE_SIZE
N_POOL = (DEC_BATCH * N_PAGES * 5) // 4

D_DA = D_MODEL // 2
D_RET = D_MODEL - D_DA
DA_HEAD_DIM = 128
DA_HEADS = D_DA // (2 * DA_HEAD_DIM)
DA_DV = 2 * DA_HEAD_DIM
RET_HEADS = 4
RET_DV = D_RET // RET_HEADS
RET_DK = RET_DV // 2
ROPE_BASE = 10000.0
EPS = 1e-6
D_IN = 4 * D_DA + 2 * RET_HEADS * RET_DK + 2 * D_RET

COL_Q, COL_K, COL_V, COL_GDA = 0, D_DA, 2 * D_DA, 3 * D_DA
COL_RQ = 4 * D_DA
COL_RK = COL_RQ + RET_HEADS * RET_DK
COL_RV = COL_RK + RET_HEADS * RET_DK
COL_GRET = COL_RV + D_RET

LANES = 128
SAMPLE_ROWS = 16
VMEM_LIMIT = 48 * 1024 * 1024

PROMPT_TM = 512
MERGE_TM = 256
N_COL_CHUNKS = 4
DEC_PAGE_GROUPS = 2
ATTN_TQ = 512
ATTN_HEADS_PER_STEP = 4
Q_SCALE = DA_HEAD_DIM ** -0.5 * math.log2(math.e)
RET_CHUNK = 128
RET_TILE = 512

LOG_DECAY = np.log(1.0 - 2.0 ** (-5.0 - np.arange(RET_HEADS, dtype=np.float32))).astype(np.float32)

F32 = jnp.float32
BF16 = jnp.bfloat16


def _cparams(n_axes):
    return pltpu.CompilerParams(dimension_semantics=("arbitrary",) * n_axes,
                                vmem_limit_bytes=VMEM_LIMIT)


def _silu(g):
    return g / (1.0 + jnp.exp(-g))


def _rms_rows(x):
    return x * lax.rsqrt(jnp.mean(x * x, axis=-1, keepdims=True) + EPS)


def _dot_nt(a, b):
    return lax.dot_general(a, b, (((1,), (1,)), ((), ())), preferred_element_type=F32)


def _dot_tn(a, b):
    return lax.dot_general(a, b, (((0,), (0,)), ((), ())), preferred_element_type=F32)


def _norm_kernel(x_ref, g_ref, h_ref):
    h_ref[...] = (_rms_rows(x_ref[...]) * g_ref[...]).astype(BF16)


def _input_norm(x, g, tm):
    t = x.shape[0]
    return pl.pallas_call(
        _norm_kernel,
        out_shape=jax.ShapeDtypeStruct((t, D_MODEL), BF16),
        grid=(t // tm,),
        in_specs=[pl.BlockSpec((tm, D_MODEL), lambda m: (m, 0)),
                  pl.BlockSpec((1, D_MODEL), lambda m: (0, 0))],
        out_specs=pl.BlockSpec((tm, D_MODEL), lambda m: (m, 0)),
        compiler_params=_cparams(1),
        name="input_norm",
    )(x, g)


def _chunk_dots(h_ref, w_ref, n_chunks, epilogue):
    width = w_ref.shape[1] // n_chunks

    def part(c):
        z = jnp.dot(h_ref[...], w_ref[:, c * width:(c + 1) * width], preferred_element_type=F32)
        epilogue(c, z)

    return [functools.partial(part, c) for c in range(n_chunks)]


def _proj_qk_parts(h_ref, w_ref, g_ref, *out_refs, scale):
    n_grp = D_DA // DA_HEAD_DIM
    grp_per_chunk = n_grp // N_COL_CHUNKS

    def epilogue(c, z):
        tm = z.shape[0]
        g = g_ref[...]
        for jj in range(grp_per_chunk):
            j = c * grp_per_chunk + jj
            y = _rms_rows(z[:, jj * DA_HEAD_DIM:(jj + 1) * DA_HEAD_DIM]) * g
            if scale is not None:
                y = y * scale
            for o in out_refs:
                if o.shape[1] == DA_HEAD_DIM:
                    o[pl.ds(j, tm, stride=n_grp), :] = y.astype(o.dtype)
                else:
                    o[:, j * DA_HEAD_DIM:(j + 1) * DA_HEAD_DIM] = y.astype(o.dtype)

    return _chunk_dots(h_ref, w_ref, N_COL_CHUNKS, epilogue)


def _proj_v_parts(h_ref, w_ref, v_ref, vact_ref):
    assert N_COL_CHUNKS == DA_HEADS

    def epilogue(h, z):
        tm = z.shape[0]
        for j in range(DA_DV // LANES):
            v_ref[pl.ds(j * DA_HEADS + h, tm, stride=2 * DA_HEADS), :] = z[:, j * LANES:(j + 1) * LANES]
        vact_ref[:, h * DA_DV:(h + 1) * DA_DV] = z.astype(vact_ref.dtype)

    return _chunk_dots(h_ref, w_ref, N_COL_CHUNKS, epilogue)


def _proj_copy_parts(h_ref, w_ref, *out_refs):
    def epilogue(c, z):
        width = z.shape[1]
        for o in out_refs:
            o[:, c * width:(c + 1) * width] = z.astype(o.dtype)

    return _chunk_dots(h_ref, w_ref, N_COL_CHUNKS, epilogue)


def _proj_gate_parts(h_ref, w_ref, o_ref):
    def epilogue(c, z):
        width = z.shape[1]
        o_ref[:, c * width:(c + 1) * width] = _silu(z).astype(o_ref.dtype)

    return _chunk_dots(h_ref, w_ref, N_COL_CHUNKS, epilogue)


def _proj_rot_parts(h_ref, w_ref, cos_ref, sin_ref, rq_ref, rk_ref):
    grp_per_chunk = 2 * RET_HEADS // N_COL_CHUNKS

    def epilogue(c, z):
        cos2 = cos_ref[...]
        sin2 = sin_ref[...]
        for jj in range(grp_per_chunk):
            j = c * grp_per_chunk + jj
            x = z[:, jj * RET_DK:(jj + 1) * RET_DK]
            if j >= RET_HEADS:
                x = x * (RET_DK ** -0.5)
            y = x * cos2 + pltpu.roll(x, RET_DK // 2, axis=1) * sin2
            if j < RET_HEADS:
                rq_ref[:, j * RET_DK:(j + 1) * RET_DK] = y.astype(rq_ref.dtype)
            else:
                jk = j - RET_HEADS
                rk_ref[:, jk * RET_DK:(jk + 1) * RET_DK] = y.astype(rk_ref.dtype)

    return _chunk_dots(h_ref, w_ref, N_COL_CHUNKS, epilogue)


def _run_parts(parts_fn):
    def kern(*refs):
        for part in parts_fn(*refs):
            part()
    return kern


def _with_carried_buffer(parts_fn):
    def wrapped(carried_ref, *refs):
        del carried_ref
        return parts_fn(*refs)
    return wrapped


def _with_decode(parts_fn, dec_kern, n_tile_in, n_dec_in, n_tile_out):
    def wrapped(pt_ref, *refs):
        tile_in = refs[:n_tile_in]
        dec_in = refs[n_tile_in:n_tile_in + n_dec_in]
        outs = refs[n_tile_in + n_dec_in:]
        dec_kern(pt_ref, *dec_in, *outs[n_tile_out:], co_parts=parts_fn(*tile_in, *outs[:n_tile_out]))
    return wrapped


def _proj_call(kern, h, w_in, layer, col0, ncols, tm, extra, extra_specs, outs, name, stacked=None,
               decode=None):
    t = h.shape[0]
    nm = t // tm
    assert col0 % ncols == 0
    cb = col0 // ncols
    out_shape = [jax.ShapeDtypeStruct((t * r, w), dt) for (r, w, dt) in outs]
    out_specs = [pl.BlockSpec((tm * r, w), lambda m, *_: (m, 0)) for (r, w, _) in outs]
    inputs = [h, w_in, *extra]
    in_specs = [pl.BlockSpec((tm, D_MODEL), lambda m, *_: (m, 0)),
                pl.BlockSpec((None, D_MODEL, ncols), lambda m, *_: (layer, 0, cb))] + extra_specs
    aliases = {}
    if stacked is not None:
        i, buf = stacked
        r, w, dt = outs[i]
        out_shape[i] = jax.ShapeDtypeStruct((DEPTH * t * r, w), dt)
        out_specs[i] = pl.BlockSpec((tm * r, w), lambda m, *_: (layer * nm + m, 0))
        if buf is not None:
            kern = _with_carried_buffer(kern)
            inputs = [buf] + inputs
            in_specs = [pl.BlockSpec(memory_space=pl.ANY)] + in_specs
            aliases = {0: i}
    return _launch_rows(kern, nm, inputs, in_specs, out_shape, out_specs, aliases, name, decode)


def _launch_rows(kern, nm, inputs, in_specs, out_shape, out_specs, aliases, name, decode):
    if decode is None:
        return pl.pallas_call(
            _run_parts(kern),
            out_shape=out_shape,
            grid=(nm,),
            in_specs=in_specs,
            out_specs=out_specs,
            input_output_aliases=aliases,
            compiler_params=_cparams(1),
            name=name,
        )(*inputs)

    job, b = decode
    npg = N_PAGES // nm
    assert npg * nm == N_PAGES
    dec = _decode_operands(job, npg, b, 1, lambda idx: (b, idx[0]))
    body = _with_decode(kern, functools.partial(_dec_attn_kernel, lam_init=job.lam_init, npg=npg, step_axis=0),
                        len(inputs), len(dec.inputs), len(out_shape))
    grid_spec = pltpu.PrefetchScalarGridSpec(
        num_scalar_prefetch=1,
        grid=(nm,),
        in_specs=in_specs + dec.in_specs,
        out_specs=out_specs + [dec.out_spec],
        scratch_shapes=dec.scratch_shapes,
    )
    return pl.pallas_call(
        body,
        out_shape=out_shape + [dec.out_shape],
        grid_spec=grid_spec,
        input_output_aliases={k + 1: v for k, v in aliases.items()},
        compiler_params=_cparams(1),
        name=name + "_dec",
    )(job.page_table, *inputs, *dec.inputs)


N_PROJ_CALLS = 7


def _project(h, w_in, layer, qn_g, kn_g, cos2, sin2, tm, act_dtype, kv_stacks=None, decode_job=None):
    k_stacked = None if kv_stacks is None else (0, kv_stacks[0])
    v_stacked = None if kv_stacks is None else (0, kv_stacks[1])
    gspec = [pl.BlockSpec((1, DA_HEAD_DIM), lambda m, *_: (0, 0))]
    cache_rows = 2 * DA_HEADS
    dec_rows = []

    def call(kern, col0, ncols, extra, extra_specs, outs, name, stacked=None):
        decode = None if decode_job is None else (decode_job, len(dec_rows))
        res = _proj_call(kern, h, w_in, layer, col0, ncols, tm, extra, extra_specs, outs, name,
                         stacked=stacked, decode=decode)
        if decode is not None:
            dec_rows.append(res[-1])
            res = res[:-1]
        return res

    (q,) = call(functools.partial(_proj_qk_parts, scale=Q_SCALE), COL_Q, D_DA, [qn_g], gspec,
                [(1, D_DA, act_dtype)], "proj_q")
    k32, kact = call(functools.partial(_proj_qk_parts, scale=None), COL_K, D_DA, [kn_g], gspec,
                     [(cache_rows, DA_HEAD_DIM, F32), (1, D_DA, act_dtype)], "proj_k", stacked=k_stacked)
    v32, vact = call(_proj_v_parts, COL_V, D_DA, [], [],
                     [(cache_rows, LANES, F32), (1, D_DA, act_dtype)], "proj_v", stacked=v_stacked)
    (g_da,) = call(_proj_gate_parts, COL_GDA, D_DA, [], [], [(1, D_DA, act_dtype)], "proj_gda")
    n_tab = cos2.shape[0] // tm
    tspec = [pl.BlockSpec((tm, RET_DK), lambda m, *_: (m % n_tab, 0))] * 2
    nrot = 2 * RET_HEADS * RET_DK
    rq, rk = call(_proj_rot_parts, COL_RQ, nrot, [cos2, sin2], tspec,
                  [(1, nrot // 2, act_dtype), (1, nrot // 2, act_dtype)], "proj_rot")
    (rv,) = call(_proj_copy_parts, COL_RV, D_RET, [], [], [(1, D_RET, act_dtype)], "proj_rv")
    (g_ret,) = call(_proj_gate_parts, COL_GRET, D_RET, [], [], [(1, D_RET, act_dtype)], "proj_gret")
    assert decode_job is None or len(dec_rows) == N_PROJ_CALLS
    return q, k32, kact, v32, vact, g_da, rq, rk, rv, g_ret, dec_rows


def _keys_from_cache_order(k32, lead):
    return k32.reshape(lead + (DA_HEADS, 2, DA_HEAD_DIM))


def _values_from_cache_order(v32, lead):
    n_half = DA_DV // LANES
    v = v32.reshape(lead + (n_half, DA_HEADS, LANES))
    nd = len(lead)
    v = v.transpose(tuple(range(nd)) + (nd + 1, nd, nd + 2))
    return v.reshape(lead + (DA_HEADS, DA_DV))


def _values_to_cache_order(cache_v):
    n_half = DA_DV // LANES
    v = cache_v.reshape(-1, PAGE_SIZE, DA_HEADS, n_half, LANES).transpose(0, 1, 3, 2, 4)
    return v.reshape(-1, PAGE_SIZE * n_half * DA_HEADS, LANES)


def _lam_value(lq1, lk1, lq2, lk2, lam_init):
    s1 = jnp.sum(lq1 * lk1, axis=-1, keepdims=True)
    s2 = jnp.sum(lq2 * lk2, axis=-1, keepdims=True)
    return jnp.exp(s1) - jnp.exp(s2) + lam_init


def _subln_gate(o, subg, gate, lam_init):
    return (_rms_rows(o) * subg) * (1.0 - lam_init) * gate.astype(F32)


def _attn_kernel(lq1_ref, lk1_ref, lq2_ref, lk2_ref, subg_ref, q_ref, k_ref, v_ref, g_ref, o_ref,
                 m_ref, l_ref, acc_ref, *, lam_init, tq, hps):
    qi = pl.program_id(2)
    q = q_ref[...]
    d = DA_HEAD_DIM

    m_ref[...] = jnp.full(m_ref.shape, -jnp.inf, F32)
    l_ref[...] = jnp.zeros(l_ref.shape, F32)
    acc_ref[...] = jnp.zeros(acc_ref.shape, F32)

    def step(r0, nr, k, v, mask):
        rows = slice(r0, r0 + nr)
        n_lt = k.shape[0] // LANES
        upd = []
        for ci in range(2 * hps):
            hh = ci // 2
            s = _dot_nt(q[rows, ci * d:(ci + 1) * d], k[:, ci * d:(ci + 1) * d])
            if mask is not None:
                s = jnp.where(mask, s, -jnp.inf)
            st = [s[:, j * LANES:(j + 1) * LANES] for j in range(n_lt)]
            fold = st[0]
            for j in range(1, n_lt):
                fold = jnp.maximum(fold, st[j])
            m_prev = m_ref[ci, rows]
            m_new = jnp.maximum(m_prev, jnp.broadcast_to(jnp.max(fold, axis=-1, keepdims=True),
                                                         (nr, LANES)))
            alpha = jnp.exp2(m_prev - m_new)
            pt = [jnp.exp2(t - m_new) for t in st]
            psum = pt[0]
            for j in range(1, n_lt):
                psum = psum + pt[j]
            p = jnp.concatenate([t.astype(BF16) for t in pt], axis=-1)
            pv = jnp.dot(p, v[:, hh * DA_DV:(hh + 1) * DA_DV], preferred_element_type=F32)
            upd.append((m_new, alpha, psum, pv))
        for ci, (m_new, alpha, psum, pv) in enumerate(upd):
            m_ref[ci, rows] = m_new
            l_ref[ci, rows] = alpha * l_ref[ci, rows] + psum
            acc_ref[ci, rows] = jnp.concatenate([alpha] * (DA_DV // LANES), axis=-1) * acc_ref[ci, rows] + pv

    def body(ki, carry):
        off = pl.multiple_of(ki * tq, tq)
        step(0, tq, k_ref[pl.ds(off, tq), :], v_ref[pl.ds(off, tq), :], None)
        return carry

    lax.fori_loop(0, qi, body, 0)

    off = pl.multiple_of(qi * tq, tq)
    hq = tq // 2
    for r0, nk in ((0, hq), (hq, tq)):
        row = lax.broadcasted_iota(jnp.int32, (hq, nk), 0) + r0
        col = lax.broadcasted_iota(jnp.int32, (hq, nk), 1)
        step(r0, hq, k_ref[pl.ds(off, nk), :], v_ref[pl.ds(off, nk), :], col <= row)

    lam = _lam_value(lq1_ref[...], lk1_ref[...], lq2_ref[...], lk2_ref[...], lam_init)
    for hh in range(hps):
        cols = slice(hh * DA_DV, (hh + 1) * DA_DV)
        l1 = jnp.sum(l_ref[2 * hh], axis=-1, keepdims=True)
        l2 = jnp.sum(l_ref[2 * hh + 1], axis=-1, keepdims=True)
        o = acc_ref[2 * hh] / l1 - lam * (acc_ref[2 * hh + 1] / l2)
        o_ref[:, cols] = _subln_gate(o, subg_ref[...], g_ref[:, cols], lam_init).astype(o_ref.dtype)


def _prompt_attention(q, k, v, g_da, lam_vecs, subg, lam_init):
    tq = ATTN_TQ
    nq = SEQ // tq
    hps = ATTN_HEADS_PER_STEP
    width = hps * DA_DV
    vec = pl.BlockSpec((1, DA_HEAD_DIM), lambda b, h, i: (0, 0))
    return pl.pallas_call(
        functools.partial(_attn_kernel, lam_init=lam_init, tq=tq, hps=hps),
        out_shape=jax.ShapeDtypeStruct((BATCH * SEQ, D_DA), BF16),
        grid=(BATCH, DA_HEADS // hps, nq),
        in_specs=[vec, vec, vec, vec,
                  pl.BlockSpec((1, DA_DV), lambda b, h, i: (0, 0)),
                  pl.BlockSpec((tq, width), lambda b, h, i: (b * nq + i, h)),
                  pl.BlockSpec((SEQ, width), lambda b, h, i: (b, h)),
                  pl.BlockSpec((SEQ, width), lambda b, h, i: (b, h)),
                  pl.BlockSpec((tq, width), lambda b, h, i: (b * nq + i, h))],
        out_specs=pl.BlockSpec((tq, width), lambda b, h, i: (b * nq + i, h)),
        scratch_shapes=[pltpu.VMEM((2 * hps, tq, LANES), F32), pltpu.VMEM((2 * hps, tq, LANES), F32),
                        pltpu.VMEM((2 * hps, tq, DA_DV), F32)],
        compiler_params=_cparams(3),
        name="prompt_attention",
    )(*lam_vecs, subg, q, k, v, g_da)


def _ret_kernel(q_ref, k_ref, v_ref, g_ref, o_ref, st_ref, *, chunk):
    c = chunk
    n_chunks = q_ref.shape[0] // c

    @pl.when(pl.program_id(1) == 0)
    def _():
        st_ref[...] = jnp.zeros(st_ref.shape, F32)

    li = lax.broadcasted_iota(jnp.int32, (c, 1), 0).astype(F32)
    diff = (lax.broadcasted_iota(jnp.int32, (c, c), 0)
            - lax.broadcasted_iota(jnp.int32, (c, c), 1)).astype(F32)
    decays = []
    for h in range(RET_HEADS):
        lg = float(LOG_DECAY[h])
        decays.append((jnp.where(diff >= 0, jnp.exp(lg * jnp.maximum(diff, 0.0)), 0.0),
                       jnp.exp((li + 1.0) * lg), jnp.exp((c - 1.0 - li) * lg), math.exp(c * lg)))

    def body(i, carry):
        off = pl.multiple_of(i * c, c)
        for h in range(RET_HEADS):
            dmask, cross_dec, k_dec, st_dec = decays[h]
            q = q_ref[pl.ds(off, c), h * RET_DK:(h + 1) * RET_DK]
            k = k_ref[pl.ds(off, c), h * RET_DK:(h + 1) * RET_DK]
            v = v_ref[pl.ds(off, c), h * RET_DV:(h + 1) * RET_DV]
            state = st_ref[h]
            s = _dot_nt(q, k) * dmask
            intra = jnp.dot(s.astype(BF16), v, preferred_element_type=F32)
            cross = jnp.dot(q, state.astype(BF16), preferred_element_type=F32) * cross_dec
            kd = (k.astype(F32) * k_dec).astype(BF16)
            st_ref[h] = state * st_dec + _dot_tn(kd, v)
            gate = g_ref[pl.ds(off, c), h * RET_DV:(h + 1) * RET_DV].astype(F32)
            o_ref[pl.ds(off, c), h * RET_DV:(h + 1) * RET_DV] = (
                _rms_rows(intra + cross) * gate).astype(o_ref.dtype)
        return carry

    lax.fori_loop(0, n_chunks, body, 0)


def _prompt_retention(rq, rk, rv, g_ret):
    ts = RET_TILE
    ns = SEQ // ts
    nqk = RET_HEADS * RET_DK
    return pl.pallas_call(
        functools.partial(_ret_kernel, chunk=RET_CHUNK),
        out_shape=[jax.ShapeDtypeStruct((BATCH * SEQ, D_RET), BF16),
                   jax.ShapeDtypeStruct((BATCH, RET_HEADS, RET_DK, RET_DV), F32)],
        grid=(BATCH, ns),
        in_specs=[pl.BlockSpec((ts, nqk), lambda b, i: (b * ns + i, 0)),
                  pl.BlockSpec((ts, nqk), lambda b, i: (b * ns + i, 0)),
                  pl.BlockSpec((ts, D_RET), lambda b, i: (b * ns + i, 0)),
                  pl.BlockSpec((ts, D_RET), lambda b, i: (b * ns + i, 0))],
        out_specs=[pl.BlockSpec((ts, D_RET), lambda b, i: (b * ns + i, 0)),
                   pl.BlockSpec((None, RET_HEADS, RET_DK, RET_DV), lambda b, i: (b, 0, 0, 0))],
        compiler_params=_cparams(2),
        name="prompt_retention",
    )(rq, rk, rv, g_ret)


def _merge_parts(a_ref, r_ref, w_ref, x_ref, g_ref, y_ref, *h_ref):
    width = D_MODEL // N_COL_CHUNKS

    def part(c):
        cols = slice(c * width, (c + 1) * width)
        ar = jnp.concatenate([a_ref[...], r_ref[...]], axis=-1)
        y_ref[:, cols] = x_ref[:, cols] + jnp.dot(ar, w_ref[:, cols], preferred_element_type=F32)

    def norm_part():
        h_ref[0][...] = (_rms_rows(y_ref[...]) * g_ref[...]).astype(BF16)

    parts = [functools.partial(part, c) for c in range(N_COL_CHUNKS)]
    return parts + [norm_part] if h_ref else parts


def _merge(a, r, w_out, layer, x, g_next, tm, decode=None):
    t = x.shape[0]
    with_h = g_next is not None
    out_shape = [jax.ShapeDtypeStruct((t, D_MODEL), F32)]
    out_specs = [pl.BlockSpec((tm, D_MODEL), lambda m, *_: (m, 0))]
    if with_h:
        out_shape.append(jax.ShapeDtypeStruct((t, D_MODEL), BF16))
        out_specs.append(pl.BlockSpec((tm, D_MODEL), lambda m, *_: (m, 0)))
    else:
        g_next = jnp.ones((1, D_MODEL), F32)
    in_specs = [pl.BlockSpec((tm, D_DA), lambda m, *_: (m, 0)),
                pl.BlockSpec((tm, D_RET), lambda m, *_: (m, 0)),
                pl.BlockSpec((None, D_MODEL, D_MODEL), lambda m, *_: (layer, 0, 0)),
                pl.BlockSpec((tm, D_MODEL), lambda m, *_: (m, 0)),
                pl.BlockSpec((1, D_MODEL), lambda m, *_: (0, 0))]
    res = list(_launch_rows(_merge_parts, t // tm, [a, r, w_out, x, g_next], in_specs, out_shape, out_specs,
                            {}, "merge", decode))
    dec_row = res.pop() if decode is not None else None
    return res[0], (res[1] if with_h else None), dec_row


def _dec_attn_kernel(pt_ref, lq1_ref, lk1_ref, lq2_ref, lk2_ref, subg_ref,
                     q_ref, kn_ref, vn_ref, g_ref, *rest, lam_init, npg, step_axis, co_parts):
    del pt_ref
    k_refs = rest[:npg]
    v_refs = rest[npg:2 * npg]
    o_ref = rest[2 * npg]
    qm_ref, m_ref, l_ref, acc_ref = rest[2 * npg + 1:]
    p_idx = pl.program_id(step_axis)
    rows = SAMPLE_ROWS

    @pl.when(p_idx == 0)
    def _():
        r = lax.broadcasted_iota(jnp.int32, (rows, D_DA), 0)
        cgrp = lax.broadcasted_iota(jnp.int32, (rows, D_DA), 1) // DA_HEAD_DIM
        qb = jnp.broadcast_to(q_ref[...], (rows, D_DA))
        qm_ref[...] = jnp.where(r == cgrp, qb, 0.0).astype(BF16)
        m_ref[...] = jnp.full(m_ref.shape, -jnp.inf, F32)
        l_ref[...] = jnp.zeros(l_ref.shape, F32)
        acc_ref[...] = jnp.zeros(acc_ref.shape, F32)

    def score_part(ks, ntok_valid):
        qm = qm_ref[...]
        s = jnp.concatenate([_dot_nt(qm, k) for k in ks], axis=-1)
        if ntok_valid is not None:
            tcol = lax.broadcasted_iota(jnp.int32, s.shape, 1)
            s = jnp.where(tcol < ntok_valid, s, -jnp.inf)
        m_prev = m_ref[...]
        m_new = jnp.maximum(m_prev, jnp.max(s, axis=-1, keepdims=True))
        alpha = jnp.exp2(m_prev - m_new)
        p = jnp.exp2(s - m_new)
        l_ref[...] = alpha * l_ref[...] + jnp.sum(p, axis=-1, keepdims=True)
        m_ref[...] = m_new
        return alpha, p.astype(BF16)

    def value_part(alpha, pb, vs):
        ntok = pb.shape[1] // len(vs)
        pv = jnp.dot(pb[:, :ntok], vs[0], preferred_element_type=F32)
        for i in range(1, len(vs)):
            pv = pv + jnp.dot(pb[:, i * ntok:(i + 1) * ntok], vs[i], preferred_element_type=F32)
        acc_ref[...] = alpha * acc_ref[...] + pv

    def load_k(r):
        n_hc = 2 * DA_HEADS
        return jnp.concatenate(
            [r[pl.ds(hc, PAGE_SIZE, stride=n_hc), :].astype(BF16) for hc in range(n_hc)], axis=-1)

    def load_v(r):
        n_half = DA_DV // LANES
        return jnp.concatenate(
            [r[pl.ds(j * DA_HEADS + h, PAGE_SIZE, stride=n_half * DA_HEADS), :].astype(BF16)
             for h in range(DA_HEADS) for j in range(n_half)], axis=-1)

    assert npg % DEC_PAGE_GROUPS == 0
    per = npg // DEC_PAGE_GROUPS
    carried = {}
    dec_parts = []
    for gi in range(DEC_PAGE_GROUPS):
        grp = slice(gi * per, (gi + 1) * per)

        def score(grp=grp):
            carried["alpha"], carried["p"] = score_part([load_k(r) for r in k_refs[grp]], None)

        def value(grp=grp):
            value_part(carried["alpha"], carried["p"], [load_v(r) for r in v_refs[grp]])

        dec_parts += [score, value]
    co_parts = list(co_parts)
    for i in range(max(len(co_parts), len(dec_parts))):
        if i < len(co_parts):
            co_parts[i]()
        if i < len(dec_parts):
            dec_parts[i]()

    @pl.when(p_idx == pl.num_programs(step_axis) - 1)
    def _():
        r = lax.broadcasted_iota(jnp.int32, (rows, D_DA), 0)
        kn = jnp.where(r == 0, jnp.broadcast_to(kn_ref[...], (rows, D_DA)), 0.0).astype(BF16)
        vn = jnp.where(r == 0, jnp.broadcast_to(vn_ref[...], (rows, D_DA)), 0.0).astype(BF16)
        value_part(*score_part([kn], 1), [vn])
        lam = _lam_value(lq1_ref[...], lk1_ref[...], lq2_ref[...], lk2_ref[...], lam_init)
        o = acc_ref[...] / l_ref[...]
        g = g_ref[...]
        subg = subg_ref[...]
        for h in range(DA_HEADS):
            sl = slice(h * DA_DV, (h + 1) * DA_DV)
            att = o[2 * h:2 * h + 1, sl] - lam * o[2 * h + 1:2 * h + 2, sl]
            o_ref[:, sl] = _subln_gate(att, subg, g[:, sl], lam_init)


class _DecodeJob(NamedTuple):
    page_table: jax.Array
    cache_k: jax.Array
    cache_v: jax.Array
    q: jax.Array
    k_new: jax.Array
    v_new: jax.Array
    g_da: jax.Array
    lam_vecs: list
    subg: jax.Array
    lam_init: float
    layer: int


class _DecodeOperands(NamedTuple):
    inputs: list
    in_specs: list
    out_shape: jax.ShapeDtypeStruct
    out_spec: pl.BlockSpec
    scratch_shapes: list


def _decode_operands(job, npg, first, n_seq, seq_and_step):
    def const(shape):
        return pl.BlockSpec(shape, lambda *a: (0,) * len(shape))

    row = pl.BlockSpec((None, 1, D_DA), lambda *a: (seq_and_step(a[:-1])[0], 0, 0))

    def page_spec(i):
        def index(*a):
            b, step = seq_and_step(a[:-1])
            return (job.layer * N_POOL + a[-1][b, step * npg + i], 0, 0)
        return pl.BlockSpec((None, PAGE_SIZE * 2 * DA_HEADS, LANES), index)

    pages = [page_spec(i) for i in range(npg)]
    return _DecodeOperands(
        inputs=[*job.lam_vecs, job.subg, job.q, job.k_new, job.v_new, job.g_da,
                *([job.cache_k] * npg), *([job.cache_v] * npg)],
        in_specs=[const((1, DA_HEAD_DIM))] * 4 + [const((1, DA_DV))] + [row] * 4 + pages + pages,
        out_shape=jax.ShapeDtypeStruct((n_seq, 1, D_DA), F32),
        out_spec=pl.BlockSpec((None, 1, D_DA), lambda *a: (seq_and_step(a[:-1])[0] - first, 0, 0)),
        scratch_shapes=[pltpu.VMEM((SAMPLE_ROWS, D_DA), BF16),
                        pltpu.VMEM((SAMPLE_ROWS, 1), F32), pltpu.VMEM((SAMPLE_ROWS, 1), F32),
                        pltpu.VMEM((SAMPLE_ROWS, D_DA), F32)])


def _dec_ret_kernel(st_ref, qc_ref, kc_ref, v_ref, g_ref, nst_ref, o_ref):
    v_all = v_ref[...]
    g_all = g_ref[...]
    for h in range(RET_HEADS):
        dec = float(1.0 - 2.0 ** (-5.0 - h))
        sl = slice(h * RET_DV, (h + 1) * RET_DV)
        st = st_ref[h]
        qc = qc_ref[h]
        kc = kc_ref[h]
        v = v_all[:, sl]
        nst_ref[h] = st * dec + kc * v
        cross = jnp.sum(qc * st, axis=0, keepdims=True) * dec
        intra = jnp.sum(qc * kc, axis=0, keepdims=True) * v
        o_ref[:, sl] = _rms_rows(intra + cross) * g_all[:, sl]


def _decode_retention(state, rq_col, rk_col, rv, g_ret, layer):
    col = pl.BlockSpec((None, RET_HEADS, RET_DK, 1), lambda b: (b, 0, 0, 0))
    row = pl.BlockSpec((None, 1, D_RET), lambda b: (b, 0, 0))
    return pl.pallas_call(
        _dec_ret_kernel,
        out_shape=[jax.ShapeDtypeStruct((DEC_BATCH, RET_HEADS, RET_DK, RET_DV), F32),
                   jax.ShapeDtypeStruct((DEC_BATCH, 1, D_RET), F32)],
        grid=(DEC_BATCH,),
        in_specs=[pl.BlockSpec((None, None, RET_HEADS, RET_DK, RET_DV), lambda b: (layer, b, 0, 0, 0)),
                  col, col, row, row],
        out_specs=[pl.BlockSpec((None, RET_HEADS, RET_DK, RET_DV), lambda b: (b, 0, 0, 0)),
                   pl.BlockSpec((None, 1, D_RET), lambda b: (b, 0, 0))],
        compiler_params=_cparams(1),
        name="decode_retention",
    )(state, rq_col, rk_col, rv, g_ret)


def _rotary_tables(pos):
    half = RET_DK // 2
    theta = 1.0 / (ROPE_BASE ** jnp.linspace(0.0, 1.0, half, dtype=F32))
    ang = pos.astype(F32)[:, None] * theta[None, :]
    cos, sin = jnp.cos(ang), jnp.sin(ang)
    return jnp.concatenate([cos, cos], axis=-1), jnp.concatenate([-sin, sin], axis=-1)


def _pad_rows(x):
    return jnp.pad(x, ((0, SAMPLE_ROWS - x.shape[0]), (0, 0)))


def kernel(x_prompt, x_sample, cache_k, cache_v, state_ret, page_table, norm_g, w_in, w_out,
           qn_g, kn_g, lam_q1, lam_k1, lam_q2, lam_k2, subln_g):
    t_p = BATCH * SEQ
    w_in_bf = w_in.astype(BF16)
    w_out_bf = w_out.astype(BF16)
    cos_p, sin_p = _rotary_tables(jnp.arange(SEQ))
    cos_s, sin_s = _rotary_tables(jnp.full((SAMPLE_ROWS,), PAST_LEN))
    ck = cache_k.reshape(DEPTH * N_POOL, PAGE_SIZE * DA_HEADS * 2, DA_HEAD_DIM)
    cv = _values_to_cache_order(cache_v)

    xp = x_prompt.reshape(t_p, D_MODEL)
    xs = _pad_rows(x_sample.reshape(DEC_BATCH, D_MODEL))
    hp = _input_norm(xp, norm_g[0][None], PROMPT_TM)
    hs = _input_norm(xs, norm_g[0][None], SAMPLE_ROWS)

    kp_all = vp_all = None
    sp_l, ks_l, vs_l, ss_l = [], [], [], []
    for l in range(DEPTH):
        lam_init = 0.8 - 0.6 * math.exp(-0.3 * l)
        lam_vecs = [a[l][None] for a in (lam_q1, lam_k1, lam_q2, lam_k2)]
        subg = subln_g[l][None]
        g_next = norm_g[l + 1][None] if l + 1 < DEPTH else None

        qs, k32, k_row, v32, v_row, g_da_s, rq_s, rk_s, rv_s, g_ret_s, _ = _project(
            hs, w_in_bf, l, qn_g[l][None], kn_g[l][None], cos_s, sin_s, SAMPLE_ROWS, F32)
        as_row = lambda z: z[:DEC_BATCH].reshape(DEC_BATCH, 1, z.shape[-1])
        as_col = lambda z: z[:DEC_BATCH].reshape(DEC_BATCH, RET_HEADS, RET_DK, 1)
        job = _DecodeJob(page_table, ck, cv, as_row(qs), as_row(k_row), as_row(v_row), as_row(g_da_s),
                         lam_vecs, subg, lam_init, l)

        q, kp_all, kbf, vp_all, vbf, g_da, rq, rk, rv, g_ret, dec_rows = _project(
            hp, w_in_bf, l, qn_g[l][None], kn_g[l][None], cos_p, sin_p, PROMPT_TM, BF16,
            kv_stacks=(kp_all, vp_all), decode_job=job)
        a = _prompt_attention(q, kbf, vbf, g_da, lam_vecs, subg, lam_init)
        r, st = _prompt_retention(rq, rk, rv, g_ret)
        xp, hp, dec_row = _merge(a, r, w_out_bf, l, xp, g_next, MERGE_TM, decode=(job, len(dec_rows)))
        dec_rows.append(dec_row)
        sp_l.append(st)

        assert len(dec_rows) == DEC_BATCH
        a = jnp.concatenate(dec_rows, axis=0)
        rq, rk, rv, g_ret = rq_s, rk_s, rv_s, g_ret_s
        nst, r = _decode_retention(state_ret, as_col(rq), as_col(rk), as_row(rv), as_row(g_ret), l)
        a16 = _pad_rows(a.reshape(DEC_BATCH, D_DA)).astype(BF16)
        r16 = _pad_rows(r.reshape(DEC_BATCH, D_RET)).astype(BF16)
        xs, hs, _ = _merge(a16, r16, w_out_bf, l, xs, g_next, SAMPLE_ROWS)
        ks_l.append(_keys_from_cache_order(k32, (SAMPLE_ROWS, DEC_SEQ))[:DEC_BATCH])
        vs_l.append(_values_from_cache_order(v32, (SAMPLE_ROWS, DEC_SEQ))[:DEC_BATCH])
        ss_l.append(nst)

    return (xp.reshape(BATCH, SEQ, D_MODEL),
            xs[:DEC_BATCH].reshape(DEC_BATCH, DEC_SEQ, D_MODEL),
            _keys_from_cache_order(kp_all, (DEPTH, BATCH, SEQ)),
            _values_from_cache_order(vp_all, (DEPTH, BATCH, SEQ)), jnp.stack(sp_l),
            jnp.stack(ks_l), jnp.stack(vs_l), jnp.stack(ss_l))
```

```python
import functools
import math
from typing import NamedTuple

import numpy as np
import jax
import jax.numpy as jnp
from jax import lax
from jax.experimental import pallas as pl
from jax.experimental.pallas import tpu as pltpu

D_MODEL = 2048
BATCH = 4
SEQ = 2048
DEPTH = 4
DEC_BATCH = 8
DEC_SEQ = 1
PAST_LEN = 16384
PAGE_SIZE = 128
N_PAGES = PAST_LEN // PAGE_SIZE
N_POOL = (DEC_BATCH * N_PAGES * 5) // 4

D_DA = D_MODEL // 2
D_RET = D_MODEL - D_DA
DA_HEAD_DIM = 128
DA_HEADS = D_DA // (2 * DA_HEAD_DIM)
DA_DV = 2 * DA_HEAD_DIM
RET_HEADS = 4
RET_DV = D_RET // RET_HEADS
RET_DK = RET_DV // 2
ROPE_BASE = 10000.0
EPS = 1e-6
D_IN = 4 * D_DA + 2 * RET_HEADS * RET_DK + 2 * D_RET

COL_Q, COL_K, COL_V, COL_GDA = 0, D_DA, 2 * D_DA, 3 * D_DA
COL_RQ = 4 * D_DA
COL_RK = COL_RQ + RET_HEADS * RET_DK
COL_RV = COL_RK + RET_HEADS * RET_DK
COL_GRET = COL_RV + D_RET

LANES = 128
SAMPLE_ROWS = 16
VMEM_LIMIT = 48 * 1024 * 1024

PROMPT_TM = 512
MERGE_TM = 256
N_COL_CHUNKS = 4
DEC_PAGE_GROUPS = 2
ATTN_TQ = 512
ATTN_HEADS_PER_STEP = 4
Q_SCALE = DA_HEAD_DIM ** -0.5 * math.log2(math.e)
RET_CHUNK = 128
RET_TILE = 512

LOG_DECAY = np.log(1.0 - 2.0 ** (-5.0 - np.arange(RET_HEADS, dtype=np.float32))).astype(np.float32)

F32 = jnp.float32
BF16 = jnp.bfloat16


def _cparams(n_axes):
    return pltpu.CompilerParams(dimension_semantics=("arbitrary",) * n_axes,
                                vmem_limit_bytes=VMEM_LIMIT)


def _silu(g):
    return g / (1.0 + jnp.exp(-g))


def _rms_rows(x):
    return x * lax.rsqrt(jnp.mean(x * x, axis=-1, keepdims=True) + EPS)


def _dot_nt(a, b):
    return lax.dot_general(a, b, (((1,), (1,)), ((), ())), preferred_element_type=F32)


def _dot_tn(a, b):
    return lax.dot_general(a, b, (((0,), (0,)), ((), ())), preferred_element_type=F32)


def _norm_kernel(x_ref, g_ref, h_ref):
    h_ref[...] = (_rms_rows(x_ref[...]) * g_ref[...]).astype(BF16)


def _input_norm(x, g, tm):
    t = x.shape[0]
    return pl.pallas_call(
        _norm_kernel,
        out_shape=jax.ShapeDtypeStruct((t, D_MODEL), BF16),
        grid=(t // tm,),
        in_specs=[pl.BlockSpec((tm, D_MODEL), lambda m: (m, 0)),
                  pl.BlockSpec((1, D_MODEL), lambda m: (0, 0))],
        out_specs=pl.BlockSpec((tm, D_MODEL), lambda m: (m, 0)),
        compiler_params=_cparams(1),
        name="input_norm",
    )(x, g)


def _chunk_dots(h_ref, w_ref, n_chunks, epilogue):
    width = w_ref.shape[1] // n_chunks

    def part(c):
        z = jnp.dot(h_ref[...], w_ref[:, c * width:(c + 1) * width], preferred_element_type=F32)
        epilogue(c, z)

    return [functools.partial(part, c) for c in range(n_chunks)]


def _proj_qk_parts(h_ref, w_ref, g_ref, *out_refs, scale):
    n_grp = D_DA // DA_HEAD_DIM
    grp_per_chunk = n_grp // N_COL_CHUNKS

    def epilogue(c, z):
        tm = z.shape[0]
        g = g_ref[...]
        for jj in range(grp_per_chunk):
            j = c * grp_per_chunk + jj
            y = _rms_rows(z[:, jj * DA_HEAD_DIM:(jj + 1) * DA_HEAD_DIM]) * g
            if scale is not None:
                y = y * scale
            for o in out_refs:
                if o.shape[1] == DA_HEAD_DIM:
                    o[pl.ds(j, tm, stride=n_grp), :] = y.astype(o.dtype)
                else:
                    o[:, j * DA_HEAD_DIM:(j + 1) * DA_HEAD_DIM] = y.astype(o.dtype)

    return _chunk_dots(h_ref, w_ref, N_COL_CHUNKS, epilogue)


def _proj_v_parts(h_ref, w_ref, v_ref, vact_ref):
    assert N_COL_CHUNKS == DA_HEADS

    def epilogue(h, z):
        tm = z.shape[0]
        for j in range(DA_DV // LANES):
            v_ref[pl.ds(j * DA_HEADS + h, tm, stride=2 * DA_HEADS), :] = z[:, j * LANES:(j + 1) * LANES]
        vact_ref[:, h * DA_DV:(h + 1) * DA_DV] = z.astype(vact_ref.dtype)

    return _chunk_dots(h_ref, w_ref, N_COL_CHUNKS, epilogue)


def _proj_copy_parts(h_ref, w_ref, *out_refs):
    def epilogue(c, z):
        width = z.shape[1]
        for o in out_refs:
            o[:, c * width:(c + 1) * width] = z.astype(o.dtype)

    return _chunk_dots(h_ref, w_ref, N_COL_CHUNKS, epilogue)


def _proj_gate_parts(h_ref, w_ref, o_ref):
    def epilogue(c, z):
        width = z.shape[1]
        o_ref[:, c * width:(c + 1) * width] = _silu(z).astype(o_ref.dtype)

    return _chunk_dots(h_ref, w_ref, N_COL_CHUNKS, epilogue)


def _proj_rot_parts(h_ref, w_ref, cos_ref, sin_ref, rq_ref, rk_ref):
    grp_per_chunk = 2 * RET_HEADS // N_COL_CHUNKS

    def epilogue(c, z):
        cos2 = cos_ref[...]
        sin2 = sin_ref[...]
        for jj in range(grp_per_chunk):
            j = c * grp_per_chunk + jj
            x = z[:, jj * RET_DK:(jj + 1) * RET_DK]
            if j >= RET_HEADS:
                x = x * (RET_DK ** -0.5)
            y = x * cos2 + pltpu.roll(x, RET_DK // 2, axis=1) * sin2
            if j < RET_HEADS:
                rq_ref[:, j * RET_DK:(j + 1) * RET_DK] = y.astype(rq_ref.dtype)
            else:
                jk = j - RET_HEADS
                rk_ref[:, jk * RET_DK:(jk + 1) * RET_DK] = y.astype(rk_ref.dtype)

    return _chunk_dots(h_ref, w_ref, N_COL_CHUNKS, epilogue)


def _run_parts(parts_fn):
    def kern(*refs):
        for part in parts_fn(*refs):
            part()
    return kern


def _with_carried_buffer(parts_fn):
    def wrapped(carried_ref, *refs):
        del carried_ref
        return parts_fn(*refs)
    return wrapped


def _with_decode(parts_fn, dec_kern, n_tile_in, n_dec_in, n_tile_out):
    def wrapped(pt_ref, *refs):
        tile_in = refs[:n_tile_in]
        dec_in = refs[n_tile_in:n_tile_in + n_dec_in]
        outs = refs[n_tile_in + n_dec_in:]
        dec_kern(pt_ref, *dec_in, *outs[n_tile_out:], co_parts=parts_fn(*tile_in, *outs[:n_tile_out]))
    return wrapped


def _proj_call(kern, h, w_in, layer, col0, ncols, tm, extra, extra_specs, outs, name, stacked=None,
               decode=None):
    t = h.shape[0]
    nm = t // tm
    assert col0 % ncols == 0
    cb = col0 // ncols
    out_shape = [jax.ShapeDtypeStruct((t * r, w), dt) for (r, w, dt) in outs]
    out_specs = [pl.BlockSpec((tm * r, w), lambda m, *_: (m, 0)) for (r, w, _) in outs]
    inputs = [h, w_in, *extra]
    in_specs = [pl.BlockSpec((tm, D_MODEL), lambda m, *_: (m, 0)),
                pl.BlockSpec((None, D_MODEL, ncols), lambda m, *_: (0, 0, cb))] + extra_specs
    aliases = {}
    if stacked is not None:
        i, buf = stacked
        r, w, dt = outs[i]
        out_shape[i] = jax.ShapeDtypeStruct((DEPTH * t * r, w), dt)
        out_specs[i] = pl.BlockSpec((tm * r, w), lambda m, *_: (layer * nm + m, 0))
        if buf is not None:
            kern = _with_carried_buffer(kern)
            inputs = [buf] + inputs
            in_specs = [pl.BlockSpec(memory_space=pl.ANY)] + in_specs
            aliases = {0: i}
    return _launch_rows(kern, nm, inputs, in_specs, out_shape, out_specs, aliases, name, decode)


def _launch_rows(kern, nm, inputs, in_specs, out_shape, out_specs, aliases, name, decode):
    if decode is None:
        return pl.pallas_call(
            _run_parts(kern),
            out_shape=out_shape,
            grid=(nm,),
            in_specs=in_specs,
            out_specs=out_specs,
            input_output_aliases=aliases,
            compiler_params=_cparams(1),
            name=name,
        )(*inputs)

    job, b = decode
    npg = N_PAGES // nm
    assert npg * nm == N_PAGES
    dec = _decode_operands(job, npg, b, 1, lambda idx: (b, idx[0]))
    body = _with_decode(kern, functools.partial(_dec_attn_kernel, lam_init=job.lam_init, npg=npg, step_axis=0),
                        len(inputs), len(dec.inputs), len(out_shape))
    grid_spec = pltpu.PrefetchScalarGridSpec(
        num_scalar_prefetch=1,
        grid=(nm,),
        in_specs=in_specs + dec.in_specs,
        out_specs=out_specs + [dec.out_spec],
        scratch_shapes=dec.scratch_shapes,
    )
    return pl.pallas_call(
        body,
        out_shape=out_shape + [dec.out_shape],
        grid_spec=grid_spec,
        input_output_aliases={k + 1: v for k, v in aliases.items()},
        compiler_params=_cparams(1),
        name=name + "_dec",
    )(job.page_table, *inputs, *dec.inputs)


N_PROJ_CALLS = 7


def _project(h, w_in, layer, qn_g, kn_g, cos2, sin2, tm, act_dtype, kv_stacks=None, decode_job=None):
    k_stacked = None if kv_stacks is None else (0, kv_stacks[0])
    v_stacked = None if kv_stacks is None else (0, kv_stacks[1])
    gspec = [pl.BlockSpec((1, DA_HEAD_DIM), lambda m, *_: (0, 0))]
    cache_rows = 2 * DA_HEADS
    dec_rows = []

    def call(kern, col0, ncols, extra, extra_specs, outs, name, stacked=None):
        decode = None if decode_job is None else (decode_job, len(dec_rows))
        res = _proj_call(kern, h, w_in, layer, col0, ncols, tm, extra, extra_specs, outs, name,
                         stacked=stacked, decode=decode)
        if decode is not None:
            dec_rows.append(res[-1])
            res = res[:-1]
        return res

    (q,) = call(functools.partial(_proj_qk_parts, scale=Q_SCALE), COL_Q, D_DA, [qn_g], gspec,
                [(1, D_DA, act_dtype)], "proj_q")
    k32, kact = call(functools.partial(_proj_qk_parts, scale=None), COL_K, D_DA, [kn_g], gspec,
                     [(cache_rows, DA_HEAD_DIM, F32), (1, D_DA, act_dtype)], "proj_k", stacked=k_stacked)
    v32, vact = call(_proj_v_parts, COL_V, D_DA, [], [],
                     [(cache_rows, LANES, F32), (1, D_DA, act_dtype)], "proj_v", stacked=v_stacked)
    (g_da,) = call(_proj_gate_parts, COL_GDA, D_DA, [], [], [(1, D_DA, act_dtype)], "proj_gda")
    n_tab = cos2.shape[0] // tm
    tspec = [pl.BlockSpec((tm, RET_DK), lambda m, *_: (m % n_tab, 0))] * 2
    nrot = 2 * RET_HEADS * RET_DK
    rq, rk = call(_proj_rot_parts, COL_RQ, nrot, [cos2, sin2], tspec,
                  [(1, nrot // 2, act_dtype), (1, nrot // 2, act_dtype)], "proj_rot")
    (rv,) = call(_proj_copy_parts, COL_RV, D_RET, [], [], [(1, D_RET, act_dtype)], "proj_rv")
    (g_ret,) = call(_proj_gate_parts, COL_GRET, D_RET, [], [], [(1, D_RET, act_dtype)], "proj_gret")
    assert decode_job is None or len(dec_rows) == N_PROJ_CALLS
    return q, k32, kact, v32, vact, g_da, rq, rk, rv, g_ret, dec_rows


def _keys_from_cache_order(k32, lead):
    return k32.reshape(lead + (DA_HEADS, 2, DA_HEAD_DIM))


def _values_from_cache_order(v32, lead):
    n_half = DA_DV // LANES
    v = v32.reshape(lead + (n_half, DA_HEADS, LANES))
    nd = len(lead)
    v = v.transpose(tuple(range(nd)) + (nd + 1, nd, nd + 2))
    return v.reshape(lead + (DA_HEADS, DA_DV))


def _values_to_cache_order(cache_v):
    n_half = DA_DV // LANES
    v = cache_v.reshape(-1, PAGE_SIZE, DA_HEADS, n_half, LANES).transpose(0, 1, 3, 2, 4)
    return v.reshape(-1, PAGE_SIZE * n_half * DA_HEADS, LANES)


def _lam_value(lq1, lk1, lq2, lk2, lam_init):
    s1 = jnp.sum(lq1 * lk1, axis=-1, keepdims=True)
    s2 = jnp.sum(lq2 * lk2, axis=-1, keepdims=True)
    return jnp.exp(s1) - jnp.exp(s2) + lam_init


def _subln_gate(o, subg, gate, lam_init):
    return (_rms_rows(o) * subg) * (1.0 - lam_init) * gate.astype(F32)


def _attn_with_weight_cast(lq1_ref, lk1_ref, lq2_ref, lk2_ref, subg_ref, q_ref, k_ref, v_ref, g_ref,
                           w32_ref, o_ref, w16_ref, m_ref, l_ref, acc_ref, **kw):
    w16_ref[...] = w32_ref[...].astype(w16_ref.dtype)
    _attn_kernel(lq1_ref, lk1_ref, lq2_ref, lk2_ref, subg_ref, q_ref, k_ref, v_ref, g_ref, o_ref,
                 m_ref, l_ref, acc_ref, **kw)


def _attn_kernel(lq1_ref, lk1_ref, lq2_ref, lk2_ref, subg_ref, q_ref, k_ref, v_ref, g_ref, o_ref,
                 m_ref, l_ref, acc_ref, *, lam_init, tq, hps):
    qi = pl.program_id(2)
    q = q_ref[...]
    d = DA_HEAD_DIM

    m_ref[...] = jnp.full(m_ref.shape, -jnp.inf, F32)
    l_ref[...] = jnp.zeros(l_ref.shape, F32)
    acc_ref[...] = jnp.zeros(acc_ref.shape, F32)

    def step(r0, nr, k, v, mask):
        rows = slice(r0, r0 + nr)
        n_lt = k.shape[0] // LANES
        upd = []
        for ci in range(2 * hps):
            hh = ci // 2
            s = _dot_nt(q[rows, ci * d:(ci + 1) * d], k[:, ci * d:(ci + 1) * d])
            if mask is not None:
                s = jnp.where(mask, s, -jnp.inf)
            st = [s[:, j * LANES:(j + 1) * LANES] for j in range(n_lt)]
            fold = st[0]
            for j in range(1, n_lt):
                fold = jnp.maximum(fold, st[j])
            m_prev = m_ref[ci, rows]
            m_new = jnp.maximum(m_prev, jnp.broadcast_to(jnp.max(fold, axis=-1, keepdims=True),
                                                         (nr, LANES)))
            alpha = jnp.exp2(m_prev - m_new)
            pt = [jnp.exp2(t - m_new) for t in st]
            psum = pt[0]
            for j in range(1, n_lt):
                psum = psum + pt[j]
            p = jnp.concatenate([t.astype(BF16) for t in pt], axis=-1)
            pv = jnp.dot(p, v[:, hh * DA_DV:(hh + 1) * DA_DV], preferred_element_type=F32)
            upd.append((m_new, alpha, psum, pv))
        for ci, (m_new, alpha, psum, pv) in enumerate(upd):
            m_ref[ci, rows] = m_new
            l_ref[ci, rows] = alpha * l_ref[ci, rows] + psum
            acc_ref[ci, rows] = jnp.concatenate([alpha] * (DA_DV // LANES), axis=-1) * acc_ref[ci, rows] + pv

    def body(ki, carry):
        off = pl.multiple_of(ki * tq, tq)
        step(0, tq, k_ref[pl.ds(off, tq), :], v_ref[pl.ds(off, tq), :], None)
        return carry

    lax.fori_loop(0, qi, body, 0)

    off = pl.multiple_of(qi * tq, tq)
    hq = tq // 2
    for r0, nk in ((0, hq), (hq, tq)):
        row = lax.broadcasted_iota(jnp.int32, (hq, nk), 0) + r0
        col = lax.broadcasted_iota(jnp.int32, (hq, nk), 1)
        step(r0, hq, k_ref[pl.ds(off, nk), :], v_ref[pl.ds(off, nk), :], col <= row)

    lam = _lam_value(lq1_ref[...], lk1_ref[...], lq2_ref[...], lk2_ref[...], lam_init)
    for hh in range(hps):
        cols = slice(hh * DA_DV, (hh + 1) * DA_DV)
        l1 = jnp.sum(l_ref[2 * hh], axis=-1, keepdims=True)
        l2 = jnp.sum(l_ref[2 * hh + 1], axis=-1, keepdims=True)
        o = acc_ref[2 * hh] / l1 - lam * (acc_ref[2 * hh + 1] / l2)
        o_ref[:, cols] = _subln_gate(o, subg_ref[...], g_ref[:, cols], lam_init).astype(o_ref.dtype)


def _prompt_attention(q, k, v, g_da, lam_vecs, subg, lam_init, w_next=None):
    tq = ATTN_TQ
    nq = SEQ // tq
    hps = ATTN_HEADS_PER_STEP
    nh = DA_HEADS // hps
    width = hps * DA_DV
    vec = pl.BlockSpec((1, DA_HEAD_DIM), lambda b, h, i: (0, 0))
    kern = _attn_kernel
    inputs = [*lam_vecs, subg, q, k, v, g_da]
    in_specs = [vec, vec, vec, vec,
                pl.BlockSpec((1, DA_DV), lambda b, h, i: (0, 0)),
                pl.BlockSpec((tq, width), lambda b, h, i: (b * nq + i, h)),
                pl.BlockSpec((SEQ, width), lambda b, h, i: (b, h)),
                pl.BlockSpec((SEQ, width), lambda b, h, i: (b, h)),
                pl.BlockSpec((tq, width), lambda b, h, i: (b * nq + i, h))]
    out_shape = [jax.ShapeDtypeStruct((BATCH * SEQ, D_DA), BF16)]
    out_specs = [pl.BlockSpec((tq, width), lambda b, h, i: (b * nq + i, h))]
    if w_next is not None:
        w_in, layer = w_next
        n_steps = BATCH * nh * nq
        rows = D_MODEL // n_steps
        assert rows * n_steps == D_MODEL
        kern = _attn_with_weight_cast
        inputs.append(w_in)
        in_specs.append(pl.BlockSpec((None, rows, D_IN), lambda b, h, i: (layer, (b * nh + h) * nq + i, 0)))
        out_shape.append(jax.ShapeDtypeStruct((1, D_MODEL, D_IN), BF16))
        out_specs.append(pl.BlockSpec((None, rows, D_IN), lambda b, h, i: (0, (b * nh + h) * nq + i, 0)))
    res = pl.pallas_call(
        functools.partial(kern, lam_init=lam_init, tq=tq, hps=hps),
        out_shape=out_shape,
        grid=(BATCH, nh, nq),
        in_specs=in_specs,
        out_specs=out_specs,
        scratch_shapes=[pltpu.VMEM((2 * hps, tq, LANES), F32), pltpu.VMEM((2 * hps, tq, LANES), F32),
                        pltpu.VMEM((2 * hps, tq, DA_DV), F32)],
        compiler_params=_cparams(3),
        name="prompt_attention",
    )(*inputs)
    return (res[0], res[1]) if w_next is not None else (res[0], None)


def _ret_kernel(q_ref, k_ref, v_ref, g_ref, o_ref, st_ref, *, chunk):
    c = chunk
    n_chunks = q_ref.shape[0] // c

    @pl.when(pl.program_id(1) == 0)
    def _():
        st_ref[...] = jnp.zeros(st_ref.shape, F32)

    li = lax.broadcasted_iota(jnp.int32, (c, 1), 0).astype(F32)
    diff = (lax.broadcasted_iota(jnp.int32, (c, c), 0)
            - lax.broadcasted_iota(jnp.int32, (c, c), 1)).astype(F32)
    decays = []
    for h in range(RET_HEADS):
        lg = float(LOG_DECAY[h])
        decays.append((jnp.where(diff >= 0, jnp.exp(lg * jnp.maximum(diff, 0.0)), 0.0),
                       jnp.exp((li + 1.0) * lg), jnp.exp((c - 1.0 - li) * lg), math.exp(c * lg)))

    def body(i, carry):
        off = pl.multiple_of(i * c, c)
        for h in range(RET_HEADS):
            dmask, cross_dec, k_dec, st_dec = decays[h]
            q = q_ref[pl.ds(off, c), h * RET_DK:(h + 1) * RET_DK]
            k = k_ref[pl.ds(off, c), h * RET_DK:(h + 1) * RET_DK]
            v = v_ref[pl.ds(off, c), h * RET_DV:(h + 1) * RET_DV]
            state = st_ref[h]
            s = _dot_nt(q, k) * dmask
            intra = jnp.dot(s.astype(BF16), v, preferred_element_type=F32)
            cross = jnp.dot(q, state.astype(BF16), preferred_element_type=F32) * cross_dec
            kd = (k.astype(F32) * k_dec).astype(BF16)
            st_ref[h] = state * st_dec + _dot_tn(kd, v)
            gate = g_ref[pl.ds(off, c), h * RET_DV:(h + 1) * RET_DV].astype(F32)
            o_ref[pl.ds(off, c), h * RET_DV:(h + 1) * RET_DV] = (
                _rms_rows(intra + cross) * gate).astype(o_ref.dtype)
        return carry

    lax.fori_loop(0, n_chunks, body, 0)


def _prompt_retention(rq, rk, rv, g_ret):
    ts = RET_TILE
    ns = SEQ // ts
    nqk = RET_HEADS * RET_DK
    return pl.pallas_call(
        functools.partial(_ret_kernel, chunk=RET_CHUNK),
        out_shape=[jax.ShapeDtypeStruct((BATCH * SEQ, D_RET), BF16),
                   jax.ShapeDtypeStruct((BATCH, RET_HEADS, RET_DK, RET_DV), F32)],
        grid=(BATCH, ns),
        in_specs=[pl.BlockSpec((ts, nqk), lambda b, i: (b * ns + i, 0)),
                  pl.BlockSpec((ts, nqk), lambda b, i: (b * ns + i, 0)),
                  pl.BlockSpec((ts, D_RET), lambda b, i: (b * ns + i, 0)),
                  pl.BlockSpec((ts, D_RET), lambda b, i: (b * ns + i, 0))],
        out_specs=[pl.BlockSpec((ts, D_RET), lambda b, i: (b * ns + i, 0)),
                   pl.BlockSpec((None, RET_HEADS, RET_DK, RET_DV), lambda b, i: (b, 0, 0, 0))],
        compiler_params=_cparams(2),
        name="prompt_retention",
    )(rq, rk, rv, g_ret)


def _merge_parts(a_ref, r_ref, w_ref, x_ref, g_ref, y_ref, *h_ref):
    width = D_MODEL // N_COL_CHUNKS

    def part(c):
        cols = slice(c * width, (c + 1) * width)
        ar = jnp.concatenate([a_ref[...], r_ref[...]], axis=-1)
        y_ref[:, cols] = x_ref[:, cols] + jnp.dot(ar, w_ref[:, cols], preferred_element_type=F32)

    def norm_part():
        h_ref[0][...] = (_rms_rows(y_ref[...]) * g_ref[...]).astype(BF16)

    parts = [functools.partial(part, c) for c in range(N_COL_CHUNKS)]
    return parts + [norm_part] if h_ref else parts


def _merge(a, r, w_out, layer, x, g_next, tm, decode=None):
    t = x.shape[0]
    with_h = g_next is not None
    out_shape = [jax.ShapeDtypeStruct((t, D_MODEL), F32)]
    out_specs = [pl.BlockSpec((tm, D_MODEL), lambda m, *_: (m, 0))]
    if with_h:
        out_shape.append(jax.ShapeDtypeStruct((t, D_MODEL), BF16))
        out_specs.append(pl.BlockSpec((tm, D_MODEL), lambda m, *_: (m, 0)))
    else:
        g_next = jnp.ones((1, D_MODEL), F32)
    in_specs = [pl.BlockSpec((tm, D_DA), lambda m, *_: (m, 0)),
                pl.BlockSpec((tm, D_RET), lambda m, *_: (m, 0)),
                pl.BlockSpec((None, D_MODEL, D_MODEL), lambda m, *_: (layer, 0, 0)),
                pl.BlockSpec((tm, D_MODEL), lambda m, *_: (m, 0)),
                pl.BlockSpec((1, D_MODEL), lambda m, *_: (0, 0))]
    res = list(_launch_rows(_merge_parts, t // tm, [a, r, w_out, x, g_next], in_specs, out_shape, out_specs,
                            {}, "merge", decode))
    dec_row = res.pop() if decode is not None else None
    return res[0], (res[1] if with_h else None), dec_row


def _dec_attn_kernel(pt_ref, small_ref, *rest, lam_init, npg, step_axis, co_parts):
    del pt_ref
    q_ref, kn_ref, vn_ref, g_ref = (small_ref.at[i:i + 1, :] for i in range(4))
    lq1_ref, lk1_ref, lq2_ref, lk2_ref = (
        small_ref.at[4:5, i * DA_HEAD_DIM:(i + 1) * DA_HEAD_DIM] for i in range(4))
    subg_ref = small_ref.at[4:5, 4 * DA_HEAD_DIM:4 * DA_HEAD_DIM + DA_DV]
    k_refs = rest[:npg]
    v_refs = rest[npg:2 * npg]
    o_ref = rest[2 * npg]
    qm_ref, m_ref, l_ref, acc_ref = rest[2 * npg + 1:]
    p_idx = pl.program_id(step_axis)
    rows = SAMPLE_ROWS

    @pl.when(p_idx == 0)
    def _():
        r = lax.broadcasted_iota(jnp.int32, (rows, D_DA), 0)
        cgrp = lax.broadcasted_iota(jnp.int32, (rows, D_DA), 1) // DA_HEAD_DIM
        qb = jnp.broadcast_to(q_ref[...], (rows, D_DA))
        qm_ref[...] = jnp.where(r == cgrp, qb, 0.0).astype(BF16)
        m_ref[...] = jnp.full(m_ref.shape, -jnp.inf, F32)
        l_ref[...] = jnp.zeros(l_ref.shape, F32)
        acc_ref[...] = jnp.zeros(acc_ref.shape, F32)

    def score_part(ks, ntok_valid):
        qm = qm_ref[...]
        s = jnp.concatenate([_dot_nt(qm, k) for k in ks], axis=-1)
        if ntok_valid is not None:
            tcol = lax.broadcasted_iota(jnp.int32, s.shape, 1)
            s = jnp.where(tcol < ntok_valid, s, -jnp.inf)
        m_prev = m_ref[...]
        m_new = jnp.maximum(m_prev, jnp.max(s, axis=-1, keepdims=True))
        alpha = jnp.exp2(m_prev - m_new)
        p = jnp.exp2(s - m_new)
        l_ref[...] = alpha * l_ref[...] + jnp.sum(p, axis=-1, keepdims=True)
        m_ref[...] = m_new
        return alpha, p.astype(BF16)

    def value_part(alpha, pb, vs):
        ntok = pb.shape[1] // len(vs)
        pv = jnp.dot(pb[:, :ntok], vs[0], preferred_element_type=F32)
        for i in range(1, len(vs)):
            pv = pv + jnp.dot(pb[:, i * ntok:(i + 1) * ntok], vs[i], preferred_element_type=F32)
        acc_ref[...] = alpha * acc_ref[...] + pv

    def load_k(r):
        n_hc = 2 * DA_HEADS
        return jnp.concatenate(
            [r[pl.ds(hc, PAGE_SIZE, stride=n_hc), :].astype(BF16) for hc in range(n_hc)], axis=-1)

    def load_v(r):
        n_half = DA_DV // LANES
        return jnp.concatenate(
            [r[pl.ds(j * DA_HEADS + h, PAGE_SIZE, stride=n_half * DA_HEADS), :].astype(BF16)
             for h in range(DA_HEADS) for j in range(n_half)], axis=-1)

    assert npg % DEC_PAGE_GROUPS == 0
    per = npg // DEC_PAGE_GROUPS
    carried = {}
    dec_parts = []
    for gi in range(DEC_PAGE_GROUPS):
        grp = slice(gi * per, (gi + 1) * per)

        def score(grp=grp):
            carried["alpha"], carried["p"] = score_part([load_k(r) for r in k_refs[grp]], None)

        def value(grp=grp):
            value_part(carried["alpha"], carried["p"], [load_v(r) for r in v_refs[grp]])

        dec_parts += [score, value]
    co_parts = list(co_parts)
    for i in range(max(len(co_parts), len(dec_parts))):
        if i < len(co_parts):
            co_parts[i]()
        if i < len(dec_parts):
            dec_parts[i]()

    @pl.when(p_idx == pl.num_programs(step_axis) - 1)
    def _():
        r = lax.broadcasted_iota(jnp.int32, (rows, D_DA), 0)
        kn = jnp.where(r == 0, jnp.broadcast_to(kn_ref[...], (rows, D_DA)), 0.0).astype(BF16)
        vn = jnp.where(r == 0, jnp.broadcast_to(vn_ref[...], (rows, D_DA)), 0.0).astype(BF16)
        value_part(*score_part([kn], 1), [vn])
        lam = _lam_value(lq1_ref[...], lk1_ref[...], lq2_ref[...], lk2_ref[...], lam_init)
        o = acc_ref[...] / l_ref[...]
        g = g_ref[...]
        subg = subg_ref[...]
        for h in range(DA_HEADS):
            sl = slice(h * DA_DV, (h + 1) * DA_DV)
            att = o[2 * h:2 * h + 1, sl] - lam * o[2 * h + 1:2 * h + 2, sl]
            o_ref[:, sl] = _subln_gate(att, subg, g[:, sl], lam_init)


class _DecodeJob(NamedTuple):
    page_table: jax.Array
    cache_k: jax.Array
    cache_v: jax.Array
    small: jax.Array
    lam_init: float
    layer: int


DEC_SMALL_ROWS = 8


def _pack_decode_rows(q, k_new, v_new, g_da, lam_vecs, subg):
    shared = jnp.concatenate([*lam_vecs, subg], axis=-1)
    shared = jnp.pad(shared, ((0, 0), (0, D_DA - shared.shape[1])))
    rows = [q, k_new, v_new, g_da, jnp.broadcast_to(shared, (DEC_BATCH, D_DA))]
    rows += [jnp.zeros((DEC_BATCH, D_DA), F32)] * (DEC_SMALL_ROWS - len(rows))
    return jnp.stack(rows, axis=1)


class _DecodeOperands(NamedTuple):
    inputs: list
    in_specs: list
    out_shape: jax.ShapeDtypeStruct
    out_spec: pl.BlockSpec
    scratch_shapes: list


def _decode_operands(job, npg, first, n_seq, seq_and_step):
    small = pl.BlockSpec((None, DEC_SMALL_ROWS, D_DA), lambda *a: (seq_and_step(a[:-1])[0], 0, 0))

    def page_spec(i):
        def index(*a):
            b, step = seq_and_step(a[:-1])
            return (job.layer * N_POOL + a[-1][b, step * npg + i], 0, 0)
        return pl.BlockSpec((None, PAGE_SIZE * 2 * DA_HEADS, LANES), index)

    pages = [page_spec(i) for i in range(npg)]
    return _DecodeOperands(
        inputs=[job.small, *([job.cache_k] * npg), *([job.cache_v] * npg)],
        in_specs=[small] + pages + pages,
        out_shape=jax.ShapeDtypeStruct((n_seq, 1, D_DA), F32),
        out_spec=pl.BlockSpec((None, 1, D_DA), lambda *a: (seq_and_step(a[:-1])[0] - first, 0, 0)),
        scratch_shapes=[pltpu.VMEM((SAMPLE_ROWS, D_DA), BF16),
                        pltpu.VMEM((SAMPLE_ROWS, 1), F32), pltpu.VMEM((SAMPLE_ROWS, 1), F32),
                        pltpu.VMEM((SAMPLE_ROWS, D_DA), F32)])


def _dec_ret_kernel(st_ref, qc_ref, kc_ref, v_ref, g_ref, nst_ref, o_ref):
    v_all = v_ref[...]
    g_all = g_ref[...]
    for h in range(RET_HEADS):
        dec = float(1.0 - 2.0 ** (-5.0 - h))
        sl = slice(h * RET_DV, (h + 1) * RET_DV)
        st = st_ref[h]
        qc = qc_ref[h]
        kc = kc_ref[h]
        v = v_all[:, sl]
        nst_ref[h] = st * dec + kc * v
        cross = jnp.sum(qc * st, axis=0, keepdims=True) * dec
        intra = jnp.sum(qc * kc, axis=0, keepdims=True) * v
        o_ref[:, sl] = _rms_rows(intra + cross) * g_all[:, sl]


def _decode_retention(state, rq_col, rk_col, rv, g_ret, layer):
    col = pl.BlockSpec((None, RET_HEADS, RET_DK, 1), lambda b: (b, 0, 0, 0))
    row = pl.BlockSpec((None, 1, D_RET), lambda b: (b, 0, 0))
    return pl.pallas_call(
        _dec_ret_kernel,
        out_shape=[jax.ShapeDtypeStruct((DEC_BATCH, RET_HEADS, RET_DK, RET_DV), F32),
                   jax.ShapeDtypeStruct((DEC_BATCH, 1, D_RET), F32)],
        grid=(DEC_BATCH,),
        in_specs=[pl.BlockSpec((None, None, RET_HEADS, RET_DK, RET_DV), lambda b: (layer, b, 0, 0, 0)),
                  col, col, row, row],
        out_specs=[pl.BlockSpec((None, RET_HEADS, RET_DK, RET_DV), lambda b: (b, 0, 0, 0)),
                   pl.BlockSpec((None, 1, D_RET), lambda b: (b, 0, 0))],
        compiler_params=_cparams(1),
        name="decode_retention",
    )(state, rq_col, rk_col, rv, g_ret)


def _rotary_tables(pos):
    half = RET_DK // 2
    theta = 1.0 / (ROPE_BASE ** jnp.linspace(0.0, 1.0, half, dtype=F32))
    ang = pos.astype(F32)[:, None] * theta[None, :]
    cos, sin = jnp.cos(ang), jnp.sin(ang)
    return jnp.concatenate([cos, cos], axis=-1), jnp.concatenate([-sin, sin], axis=-1)


def _pad_rows(x):
    return jnp.pad(x, ((0, SAMPLE_ROWS - x.shape[0]), (0, 0)))


def kernel(x_prompt, x_sample, cache_k, cache_v, state_ret, page_table, norm_g, w_in, w_out,
           qn_g, kn_g, lam_q1, lam_k1, lam_q2, lam_k2, subln_g):
    t_p = BATCH * SEQ
    w_in_bf = w_in[0:1].astype(BF16)
    w_out_bf = w_out.astype(BF16)
    cos_p, sin_p = _rotary_tables(jnp.arange(SEQ))
    cos_s, sin_s = _rotary_tables(jnp.full((SAMPLE_ROWS,), PAST_LEN))
    ck = cache_k.reshape(DEPTH * N_POOL, PAGE_SIZE * DA_HEADS * 2, DA_HEAD_DIM)
    cv = _values_to_cache_order(cache_v)

    xp = x_prompt.reshape(t_p, D_MODEL)
    xs = _pad_rows(x_sample.reshape(DEC_BATCH, D_MODEL))
    hp = _input_norm(xp, norm_g[0][None], PROMPT_TM)
    hs = _input_norm(xs, norm_g[0][None], SAMPLE_ROWS)

    kp_all = vp_all = None
    sp_l, ks_l, vs_l, ss_l = [], [], [], []
    for l in range(DEPTH):
        lam_init = 0.8 - 0.6 * math.exp(-0.3 * l)
        lam_vecs = [a[l][None] for a in (lam_q1, lam_k1, lam_q2, lam_k2)]
        subg = subln_g[l][None]
        g_next = norm_g[l + 1][None] if l + 1 < DEPTH else None

        qs, k32, k_row, v32, v_row, g_da_s, rq_s, rk_s, rv_s, g_ret_s, _ = _project(
            hs, w_in_bf, l, qn_g[l][None], kn_g[l][None], cos_s, sin_s, SAMPLE_ROWS, F32)
        as_row = lambda z: z[:DEC_BATCH].reshape(DEC_BATCH, 1, z.shape[-1])
        as_col = lambda z: z[:DEC_BATCH].reshape(DEC_BATCH, RET_HEADS, RET_DK, 1)
        small = _pack_decode_rows(qs[:DEC_BATCH], k_row[:DEC_BATCH], v_row[:DEC_BATCH], g_da_s[:DEC_BATCH],
                                  lam_vecs, subg)
        job = _DecodeJob(page_table, ck, cv, small, lam_init, l)

        q, kp_all, kbf, vp_all, vbf, g_da, rq, rk, rv, g_ret, dec_rows = _project(
            hp, w_in_bf, l, qn_g[l][None], kn_g[l][None], cos_p, sin_p, PROMPT_TM, BF16,
            kv_stacks=(kp_all, vp_all), decode_job=job)
        a, w_in_bf = _prompt_attention(q, kbf, vbf, g_da, lam_vecs, subg, lam_init,
                                       w_next=(w_in, l + 1) if l + 1 < DEPTH else None)
        r, st = _prompt_retention(rq, rk, rv, g_ret)
        xp, hp, dec_row = _merge(a, r, w_out_bf, l, xp, g_next, MERGE_TM, decode=(job, len(dec_rows)))
        dec_rows.append(dec_row)
        sp_l.append(st)

        assert len(dec_rows) == DEC_BATCH
        a = jnp.concatenate(dec_rows, axis=0)
        rq, rk, rv, g_ret = rq_s, rk_s, rv_s, g_ret_s
        nst, r = _decode_retention(state_ret, as_col(rq), as_col(rk), as_row(rv), as_row(g_ret), l)
        a16 = _pad_rows(a.reshape(DEC_BATCH, D_DA)).astype(BF16)
        r16 = _pad_rows(r.reshape(DEC_BATCH, D_RET)).astype(BF16)
        xs, hs, _ = _merge(a16, r16, w_out_bf, l, xs, g_next, SAMPLE_ROWS)
        ks_l.append(_keys_from_cache_order(k32, (SAMPLE_ROWS, DEC_SEQ))[:DEC_BATCH])
        vs_l.append(_values_from_cache_order(v32, (SAMPLE_ROWS, DEC_SEQ))[:DEC_BATCH])
        ss_l.append(nst)

    return (xp.reshape(BATCH, SEQ, D_MODEL),
            xs[:DEC_BATCH].reshape(DEC_BATCH, DEC_SEQ, D_MODEL),
            _keys_from_cache_order(kp_all, (DEPTH, BATCH, SEQ)),
            _values_from_cache_order(vp_all, (DEPTH, BATCH, SEQ)), jnp.stack(sp_l),
            jnp.stack(ks_l), jnp.stack(vs_l), jnp.stack(ss_l))
```

```python
import functools
import math
from typing import NamedTuple

import numpy as np
import jax
import jax.numpy as jnp
from jax import lax
from jax.experimental import pallas as pl
from jax.experimental.pallas import tpu as pltpu

D_MODEL = 2048
BATCH = 4
SEQ = 2048
DEPTH = 4
DEC_BATCH = 8
DEC_SEQ = 1
PAST_LEN = 16384
PAGE_SIZE = 128
N_PAGES = PAST_LEN // PAGE_SIZE
N_POOL = (DEC_BATCH * N_PAGES * 5) // 4

D_DA = D_MODEL // 2
D_RET = D_MODEL - D_DA
DA_HEAD_DIM = 128
DA_HEADS = D_DA // (2 * DA_HEAD_DIM)
DA_DV = 2 * DA_HEAD_DIM
RET_HEADS = 4
RET_DV = D_RET // RET_HEADS
RET_DK = RET_DV // 2
ROPE_BASE = 10000.0
EPS = 1e-6
D_IN = 4 * D_DA + 2 * RET_HEADS * RET_DK + 2 * D_RET

COL_Q, COL_K, COL_V, COL_GDA = 0, D_DA, 2 * D_DA, 3 * D_DA
COL_RQ = 4 * D_DA
COL_RK = COL_RQ + RET_HEADS * RET_DK
COL_RV = COL_RK + RET_HEADS * RET_DK
COL_GRET = COL_RV + D_RET

LANES = 128
SAMPLE_ROWS = 16
VMEM_LIMIT = 48 * 1024 * 1024

PROMPT_TM = 512
MERGE_TM = 256
N_COL_CHUNKS = 4
DEC_PAGE_GROUPS = 2
ATTN_TQ = 512
ATTN_HEADS_PER_STEP = 4
Q_SCALE = DA_HEAD_DIM ** -0.5 * math.log2(math.e)
RET_CHUNK = 128
RET_TILE = 512

LOG_DECAY = np.log(1.0 - 2.0 ** (-5.0 - np.arange(RET_HEADS, dtype=np.float32))).astype(np.float32)

F32 = jnp.float32
BF16 = jnp.bfloat16


def _cparams(n_axes):
    return pltpu.CompilerParams(dimension_semantics=("arbitrary",) * n_axes,
                                vmem_limit_bytes=VMEM_LIMIT)


def _silu(g):
    return g / (1.0 + jnp.exp(-g))


def _rms_rows(x):
    return x * lax.rsqrt(jnp.mean(x * x, axis=-1, keepdims=True) + EPS)


def _dot_nt(a, b):
    return lax.dot_general(a, b, (((1,), (1,)), ((), ())), preferred_element_type=F32)


def _dot_tn(a, b):
    return lax.dot_general(a, b, (((0,), (0,)), ((), ())), preferred_element_type=F32)


def _norm_kernel(x_ref, g_ref, h_ref):
    h_ref[...] = (_rms_rows(x_ref[...]) * g_ref[...]).astype(BF16)


def _input_norm(x, g, tm):
    t = x.shape[0]
    return pl.pallas_call(
        _norm_kernel,
        out_shape=jax.ShapeDtypeStruct((t, D_MODEL), BF16),
        grid=(t // tm,),
        in_specs=[pl.BlockSpec((tm, D_MODEL), lambda m: (m, 0)),
                  pl.BlockSpec((1, D_MODEL), lambda m: (0, 0))],
        out_specs=pl.BlockSpec((tm, D_MODEL), lambda m: (m, 0)),
        compiler_params=_cparams(1),
        name="input_norm",
    )(x, g)


def _chunk_dots(h_ref, w_ref, n_chunks, epilogue):
    width = w_ref.shape[1] // n_chunks

    def part(c):
        z = jnp.dot(h_ref[...], w_ref[:, c * width:(c + 1) * width], preferred_element_type=F32)
        epilogue(c, z)

    return [functools.partial(part, c) for c in range(n_chunks)]


def _proj_qk_parts(h_ref, w_ref, g_ref, *out_refs, scale):
    n_grp = D_DA // DA_HEAD_DIM
    grp_per_chunk = n_grp // N_COL_CHUNKS

    def epilogue(c, z):
        tm = z.shape[0]
        g = g_ref[...]
        for jj in range(grp_per_chunk):
            j = c * grp_per_chunk + jj
            y = _rms_rows(z[:, jj * DA_HEAD_DIM:(jj + 1) * DA_HEAD_DIM]) * g
            if scale is not None:
                y = y * scale
            for o in out_refs:
                if o.shape[1] == DA_HEAD_DIM:
                    o[pl.ds(j, tm, stride=n_grp), :] = y.astype(o.dtype)
                else:
                    o[:, j * DA_HEAD_DIM:(j + 1) * DA_HEAD_DIM] = y.astype(o.dtype)

    return _chunk_dots(h_ref, w_ref, N_COL_CHUNKS, epilogue)


def _proj_v_parts(h_ref, w_ref, v_ref, vact_ref):
    assert N_COL_CHUNKS == DA_HEADS

    def epilogue(h, z):
        tm = z.shape[0]
        for j in range(DA_DV // LANES):
            v_ref[pl.ds(j * DA_HEADS + h, tm, stride=2 * DA_HEADS), :] = z[:, j * LANES:(j + 1) * LANES]
        vact_ref[:, h * DA_DV:(h + 1) * DA_DV] = z.astype(vact_ref.dtype)

    return _chunk_dots(h_ref, w_ref, N_COL_CHUNKS, epilogue)


def _proj_copy_parts(h_ref, w_ref, *out_refs):
    def epilogue(c, z):
        width = z.shape[1]
        for o in out_refs:
            o[:, c * width:(c + 1) * width] = z.astype(o.dtype)

    return _chunk_dots(h_ref, w_ref, N_COL_CHUNKS, epilogue)


def _proj_gate_parts(h_ref, w_ref, o_ref):
    def epilogue(c, z):
        width = z.shape[1]
        o_ref[:, c * width:(c + 1) * width] = _silu(z).astype(o_ref.dtype)

    return _chunk_dots(h_ref, w_ref, N_COL_CHUNKS, epilogue)


def _proj_rot_parts(h_ref, w_ref, cos_ref, sin_ref, rq_ref, rk_ref):
    grp_per_chunk = 2 * RET_HEADS // N_COL_CHUNKS

    def epilogue(c, z):
        cos2 = cos_ref[...]
        sin2 = sin_ref[...]
        for jj in range(grp_per_chunk):
            j = c * grp_per_chunk + jj
            x = z[:, jj * RET_DK:(jj + 1) * RET_DK]
            if j >= RET_HEADS:
                x = x * (RET_DK ** -0.5)
            y = x * cos2 + pltpu.roll(x, RET_DK // 2, axis=1) * sin2
            if j < RET_HEADS:
                rq_ref[:, j * RET_DK:(j + 1) * RET_DK] = y.astype(rq_ref.dtype)
            else:
                jk = j - RET_HEADS
                rk_ref[:, jk * RET_DK:(jk + 1) * RET_DK] = y.astype(rk_ref.dtype)

    return _chunk_dots(h_ref, w_ref, N_COL_CHUNKS, epilogue)


def _run_parts(parts_fn):
    def kern(*refs):
        for part in parts_fn(*refs):
            part()
    return kern


def _with_carried_buffer(parts_fn):
    def wrapped(carried_ref, *refs):
        del carried_ref
        return parts_fn(*refs)
    return wrapped


def _with_decode(parts_fn, dec_kern, n_tile_in, n_dec_in, n_tile_out):
    def wrapped(pt_ref, *refs):
        tile_in = refs[:n_tile_in]
        dec_in = refs[n_tile_in:n_tile_in + n_dec_in]
        outs = refs[n_tile_in + n_dec_in:]
        dec_kern(pt_ref, *dec_in, *outs[n_tile_out:], co_parts=parts_fn(*tile_in, *outs[:n_tile_out]))
    return wrapped


def _proj_call(kern, h, w_in, layer, col0, ncols, tm, extra, extra_specs, outs, name, stacked=None,
               decode=None):
    t = h.shape[0]
    nm = t // tm
    assert col0 % ncols == 0
    cb = col0 // ncols
    out_shape = [jax.ShapeDtypeStruct((t * r, w), dt) for (r, w, dt) in outs]
    out_specs = [pl.BlockSpec((tm * r, w), lambda m, *_: (m, 0)) for (r, w, _) in outs]
    inputs = [h, w_in, *extra]
    in_specs = [pl.BlockSpec((tm, D_MODEL), lambda m, *_: (m, 0)),
                pl.BlockSpec((None, D_MODEL, ncols), lambda m, *_: (0, 0, cb))] + extra_specs
    aliases = {}
    if stacked is not None:
        i, buf = stacked
        r, w, dt = outs[i]
        out_shape[i] = jax.ShapeDtypeStruct((DEPTH * t * r, w), dt)
        out_specs[i] = pl.BlockSpec((tm * r, w), lambda m, *_: (layer * nm + m, 0))
        if buf is not None:
            kern = _with_carried_buffer(kern)
            inputs = [buf] + inputs
            in_specs = [pl.BlockSpec(memory_space=pl.ANY)] + in_specs
            aliases = {0: i}
    return _launch_rows(kern, nm, inputs, in_specs, out_shape, out_specs, aliases, name, decode)


def _launch_rows(kern, nm, inputs, in_specs, out_shape, out_specs, aliases, name, decode):
    if decode is None:
        return pl.pallas_call(
            _run_parts(kern),
            out_shape=out_shape,
            grid=(nm,),
            in_specs=in_specs,
            out_specs=out_specs,
            input_output_aliases=aliases,
            compiler_params=_cparams(1),
            name=name,
        )(*inputs)

    job, b = decode
    npg = N_PAGES // nm
    assert npg * nm == N_PAGES
    dec = _decode_operands(job, npg, b, 1, lambda idx: (b, idx[0]))
    body = _with_decode(kern, functools.partial(_dec_attn_kernel, lam_init=job.lam_init, npg=npg, step_axis=0),
                        len(inputs), len(dec.inputs), len(out_shape))
    grid_spec = pltpu.PrefetchScalarGridSpec(
        num_scalar_prefetch=1,
        grid=(nm,),
        in_specs=in_specs + dec.in_specs,
        out_specs=out_specs + [dec.out_spec],
        scratch_shapes=dec.scratch_shapes,
    )
    return pl.pallas_call(
        body,
        out_shape=out_shape + [dec.out_shape],
        grid_spec=grid_spec,
        input_output_aliases={k + 1: v for k, v in aliases.items()},
        compiler_params=_cparams(1),
        name=name + "_dec",
    )(job.page_table, *inputs, *dec.inputs)


N_PROJ_CALLS = 7


def _project(h, w_in, layer, qn_g, kn_g, cos2, sin2, tm, act_dtype, kv_stacks=None, decode_job=None):
    k_stacked = None if kv_stacks is None else (0, kv_stacks[0])
    v_stacked = None if kv_stacks is None else (0, kv_stacks[1])
    gspec = [pl.BlockSpec((1, DA_HEAD_DIM), lambda m, *_: (0, 0))]
    cache_rows = 2 * DA_HEADS
    dec_rows = []

    def call(kern, col0, ncols, extra, extra_specs, outs, name, stacked=None):
        decode = None if decode_job is None else (decode_job, len(dec_rows))
        res = _proj_call(kern, h, w_in, layer, col0, ncols, tm, extra, extra_specs, outs, name,
                         stacked=stacked, decode=decode)
        if decode is not None:
            dec_rows.append(res[-1])
            res = res[:-1]
        return res

    (q,) = call(functools.partial(_proj_qk_parts, scale=Q_SCALE), COL_Q, D_DA, [qn_g], gspec,
                [(1, D_DA, act_dtype)], "proj_q")
    k32, kact = call(functools.partial(_proj_qk_parts, scale=None), COL_K, D_DA, [kn_g], gspec,
                     [(cache_rows, DA_HEAD_DIM, F32), (1, D_DA, act_dtype)], "proj_k", stacked=k_stacked)
    v32, vact = call(_proj_v_parts, COL_V, D_DA, [], [],
                     [(cache_rows, LANES, F32), (1, D_DA, act_dtype)], "proj_v", stacked=v_stacked)
    (g_da,) = call(_proj_gate_parts, COL_GDA, D_DA, [], [], [(1, D_DA, act_dtype)], "proj_gda")
    n_tab = cos2.shape[0] // tm
    tspec = [pl.BlockSpec((tm, RET_DK), lambda m, *_: (m % n_tab, 0))] * 2
    nrot = 2 * RET_HEADS * RET_DK
    rq, rk = call(_proj_rot_parts, COL_RQ, nrot, [cos2, sin2], tspec,
                  [(1, nrot // 2, act_dtype), (1, nrot // 2, act_dtype)], "proj_rot")
    (rv,) = call(_proj_copy_parts, COL_RV, D_RET, [], [], [(1, D_RET, act_dtype)], "proj_rv")
    (g_ret,) = call(_proj_gate_parts, COL_GRET, D_RET, [], [], [(1, D_RET, act_dtype)], "proj_gret")
    assert decode_job is None or len(dec_rows) == N_PROJ_CALLS
    return q, k32, kact, v32, vact, g_da, rq, rk, rv, g_ret, dec_rows


SAMPLE_SEG = D_DA


def _sample_proj_kernel(h_ref, w_ref, qg_ref, kg_ref, cos_ref, sin_ref, o_ref):
    j = pl.program_id(0)
    z = jnp.dot(h_ref[...], w_ref[...], preferred_element_type=F32)

    def norm_groups(g, scale):
        for grp in range(SAMPLE_SEG // DA_HEAD_DIM):
            sl = slice(grp * DA_HEAD_DIM, (grp + 1) * DA_HEAD_DIM)
            y = _rms_rows(z[:, sl]) * g
            o_ref[:, sl] = y * scale if scale is not None else y

    @pl.when(j == COL_Q // SAMPLE_SEG)
    def _():
        norm_groups(qg_ref[...], Q_SCALE)

    @pl.when(j == COL_K // SAMPLE_SEG)
    def _():
        norm_groups(kg_ref[...], None)

    @pl.when((j == COL_V // SAMPLE_SEG) | (j == COL_RV // SAMPLE_SEG))
    def _():
        o_ref[...] = z

    @pl.when((j == COL_GDA // SAMPLE_SEG) | (j == COL_GRET // SAMPLE_SEG))
    def _():
        o_ref[...] = _silu(z)

    @pl.when(j == COL_RQ // SAMPLE_SEG)
    def _():
        cos2 = cos_ref[...]
        sin2 = sin_ref[...]
        for grp in range(2 * RET_HEADS):
            sl = slice(grp * RET_DK, (grp + 1) * RET_DK)
            x = z[:, sl]
            if grp >= RET_HEADS:
                x = x * (RET_DK ** -0.5)
            o_ref[:, sl] = x * cos2 + pltpu.roll(x, RET_DK // 2, axis=1) * sin2


def _sample_project(h, w_in, qn_g, kn_g, cos2, sin2):
    assert 2 * RET_HEADS * RET_DK == SAMPLE_SEG and COL_RK == COL_RQ + RET_HEADS * RET_DK
    rows = h.shape[0]
    const = lambda shape: pl.BlockSpec(shape, lambda j: (0,) * len(shape))
    return pl.pallas_call(
        _sample_proj_kernel,
        out_shape=jax.ShapeDtypeStruct((rows, D_IN), F32),
        grid=(D_IN // SAMPLE_SEG,),
        in_specs=[const((rows, D_MODEL)),
                  pl.BlockSpec((None, D_MODEL, SAMPLE_SEG), lambda j: (0, 0, j)),
                  const((1, DA_HEAD_DIM)), const((1, DA_HEAD_DIM)),
                  const((rows, RET_DK)), const((rows, RET_DK))],
        out_specs=pl.BlockSpec((rows, SAMPLE_SEG), lambda j: (0, j)),
        compiler_params=_cparams(1),
        name="sample_proj",
    )(h, w_in, qn_g, kn_g, cos2, sin2)


def _keys_from_cache_order(k32, lead):
    return k32.reshape(lead + (DA_HEADS, 2, DA_HEAD_DIM))


def _values_from_cache_order(v32, lead):
    n_half = DA_DV // LANES
    v = v32.reshape(lead + (n_half, DA_HEADS, LANES))
    nd = len(lead)
    v = v.transpose(tuple(range(nd)) + (nd + 1, nd, nd + 2))
    return v.reshape(lead + (DA_HEADS, DA_DV))


def _values_to_cache_order(cache_v):
    n_half = DA_DV // LANES
    v = cache_v.reshape(-1, PAGE_SIZE, DA_HEADS, n_half, LANES).transpose(0, 1, 3, 2, 4)
    return v.reshape(-1, PAGE_SIZE * n_half * DA_HEADS, LANES)


def _lam_value(lq1, lk1, lq2, lk2, lam_init):
    s1 = jnp.sum(lq1 * lk1, axis=-1, keepdims=True)
    s2 = jnp.sum(lq2 * lk2, axis=-1, keepdims=True)
    return jnp.exp(s1) - jnp.exp(s2) + lam_init


def _subln_gate(o, subg, gate, lam_init):
    return (_rms_rows(o) * subg) * (1.0 - lam_init) * gate.astype(F32)


def _attn_with_weight_cast(lq1_ref, lk1_ref, lq2_ref, lk2_ref, subg_ref, q_ref, k_ref, v_ref, g_ref,
                           w32_ref, o_ref, w16_ref, m_ref, l_ref, acc_ref, **kw):
    w16_ref[...] = w32_ref[...].astype(w16_ref.dtype)
    _attn_kernel(lq1_ref, lk1_ref, lq2_ref, lk2_ref, subg_ref, q_ref, k_ref, v_ref, g_ref, o_ref,
                 m_ref, l_ref, acc_ref, **kw)


def _attn_kernel(lq1_ref, lk1_ref, lq2_ref, lk2_ref, subg_ref, q_ref, k_ref, v_ref, g_ref, o_ref,
                 m_ref, l_ref, acc_ref, *, lam_init, tq, hps):
    qi = pl.program_id(2)
    q = q_ref[...]
    d = DA_HEAD_DIM

    m_ref[...] = jnp.full(m_ref.shape, -jnp.inf, F32)
    l_ref[...] = jnp.zeros(l_ref.shape, F32)
    acc_ref[...] = jnp.zeros(acc_ref.shape, F32)

    def step(r0, nr, k, v, mask):
        rows = slice(r0, r0 + nr)
        n_lt = k.shape[0] // LANES
        upd = []
        for ci in range(2 * hps):
            hh = ci // 2
            s = _dot_nt(q[rows, ci * d:(ci + 1) * d], k[:, ci * d:(ci + 1) * d])
            if mask is not None:
                s = jnp.where(mask, s, -jnp.inf)
            st = [s[:, j * LANES:(j + 1) * LANES] for j in range(n_lt)]
            fold = st[0]
            for j in range(1, n_lt):
                fold = jnp.maximum(fold, st[j])
            m_prev = m_ref[ci, rows]
            m_new = jnp.maximum(m_prev, jnp.broadcast_to(jnp.max(fold, axis=-1, keepdims=True),
                                                         (nr, LANES)))
            alpha = jnp.exp2(m_prev - m_new)
            pt = [jnp.exp2(t - m_new) for t in st]
            psum = pt[0]
            for j in range(1, n_lt):
                psum = psum + pt[j]
            p = jnp.concatenate([t.astype(BF16) for t in pt], axis=-1)
            pv = jnp.dot(p, v[:, hh * DA_DV:(hh + 1) * DA_DV], preferred_element_type=F32)
            upd.append((m_new, alpha, psum, pv))
        for ci, (m_new, alpha, psum, pv) in enumerate(upd):
            m_ref[ci, rows] = m_new
            l_ref[ci, rows] = alpha * l_ref[ci, rows] + psum
            acc_ref[ci, rows] = jnp.concatenate([alpha] * (DA_DV // LANES), axis=-1) * acc_ref[ci, rows] + pv

    def body(ki, carry):
        off = pl.multiple_of(ki * tq, tq)
        step(0, tq, k_ref[pl.ds(off, tq), :], v_ref[pl.ds(off, tq), :], None)
        return carry

    lax.fori_loop(0, qi, body, 0)

    off = pl.multiple_of(qi * tq, tq)
    hq = tq // 2
    for r0, nk in ((0, hq), (hq, tq)):
        row = lax.broadcasted_iota(jnp.int32, (hq, nk), 0) + r0
        col = lax.broadcasted_iota(jnp.int32, (hq, nk), 1)
        step(r0, hq, k_ref[pl.ds(off, nk), :], v_ref[pl.ds(off, nk), :], col <= row)

    lam = _lam_value(lq1_ref[...], lk1_ref[...], lq2_ref[...], lk2_ref[...], lam_init)
    for hh in range(hps):
        cols = slice(hh * DA_DV, (hh + 1) * DA_DV)
        l1 = jnp.sum(l_ref[2 * hh], axis=-1, keepdims=True)
        l2 = jnp.sum(l_ref[2 * hh + 1], axis=-1, keepdims=True)
        o = acc_ref[2 * hh] / l1 - lam * (acc_ref[2 * hh + 1] / l2)
        o_ref[:, cols] = _subln_gate(o, subg_ref[...], g_ref[:, cols], lam_init).astype(o_ref.dtype)


def _prompt_attention(q, k, v, g_da, lam_vecs, subg, lam_init, w_next=None):
    tq = ATTN_TQ
    nq = SEQ // tq
    hps = ATTN_HEADS_PER_STEP
    nh = DA_HEADS // hps
    width = hps * DA_DV
    vec = pl.BlockSpec((1, DA_HEAD_DIM), lambda b, h, i: (0, 0))
    kern = _attn_kernel
    inputs = [*lam_vecs, subg, q, k, v, g_da]
    in_specs = [vec, vec, vec, vec,
                pl.BlockSpec((1, DA_DV), lambda b, h, i: (0, 0)),
                pl.BlockSpec((tq, width), lambda b, h, i: (b * nq + i, h)),
                pl.BlockSpec((SEQ, width), lambda b, h, i: (b, h)),
                pl.BlockSpec((SEQ, width), lambda b, h, i: (b, h)),
                pl.BlockSpec((tq, width), lambda b, h, i: (b * nq + i, h))]
    out_shape = [jax.ShapeDtypeStruct((BATCH * SEQ, D_DA), BF16)]
    out_specs = [pl.BlockSpec((tq, width), lambda b, h, i: (b * nq + i, h))]
    if w_next is not None:
        w_in, layer = w_next
        n_steps = BATCH * nh * nq
        rows = D_MODEL // n_steps
        assert rows * n_steps == D_MODEL
        kern = _attn_with_weight_cast
        inputs.append(w_in)
        in_specs.append(pl.BlockSpec((None, rows, D_IN), lambda b, h, i: (layer, (b * nh + h) * nq + i, 0)))
        out_shape.append(jax.ShapeDtypeStruct((1, D_MODEL, D_IN), BF16))
        out_specs.append(pl.BlockSpec((None, rows, D_IN), lambda b, h, i: (0, (b * nh + h) * nq + i, 0)))
    res = pl.pallas_call(
        functools.partial(kern, lam_init=lam_init, tq=tq, hps=hps),
        out_shape=out_shape,
        grid=(BATCH, nh, nq),
        in_specs=in_specs,
        out_specs=out_specs,
        scratch_shapes=[pltpu.VMEM((2 * hps, tq, LANES), F32), pltpu.VMEM((2 * hps, tq, LANES), F32),
                        pltpu.VMEM((2 * hps, tq, DA_DV), F32)],
        compiler_params=_cparams(3),
        name="prompt_attention",
    )(*inputs)
    return (res[0], res[1]) if w_next is not None else (res[0], None)


def _ret_kernel(q_ref, k_ref, v_ref, g_ref, o_ref, st_ref, *, chunk):
    c = chunk
    n_chunks = q_ref.shape[0] // c

    @pl.when(pl.program_id(1) == 0)
    def _():
        st_ref[...] = jnp.zeros(st_ref.shape, F32)

    li = lax.broadcasted_iota(jnp.int32, (c, 1), 0).astype(F32)
    diff = (lax.broadcasted_iota(jnp.int32, (c, c), 0)
            - lax.broadcasted_iota(jnp.int32, (c, c), 1)).astype(F32)
    decays = []
    for h in range(RET_HEADS):
        lg = float(LOG_DECAY[h])
        decays.append((jnp.where(diff >= 0, jnp.exp(lg * jnp.maximum(diff, 0.0)), 0.0),
                       jnp.exp((li + 1.0) * lg), jnp.exp((c - 1.0 - li) * lg), math.exp(c * lg)))

    def body(i, carry):
        off = pl.multiple_of(i * c, c)
        for h in range(RET_HEADS):
            dmask, cross_dec, k_dec, st_dec = decays[h]
            q = q_ref[pl.ds(off, c), h * RET_DK:(h + 1) * RET_DK]
            k = k_ref[pl.ds(off, c), h * RET_DK:(h + 1) * RET_DK]
            v = v_ref[pl.ds(off, c), h * RET_DV:(h + 1) * RET_DV]
            state = st_ref[h]
            s = _dot_nt(q, k) * dmask
            intra = jnp.dot(s.astype(BF16), v, preferred_element_type=F32)
            cross = jnp.dot(q, state.astype(BF16), preferred_element_type=F32) * cross_dec
            kd = (k.astype(F32) * k_dec).astype(BF16)
            st_ref[h] = state * st_dec + _dot_tn(kd, v)
            gate = g_ref[pl.ds(off, c), h * RET_DV:(h + 1) * RET_DV].astype(F32)
            o_ref[pl.ds(off, c), h * RET_DV:(h + 1) * RET_DV] = (
                _rms_rows(intra + cross) * gate).astype(o_ref.dtype)
        return carry

    lax.fori_loop(0, n_chunks, body, 0)


def _prompt_retention(rq, rk, rv, g_ret):
    ts = RET_TILE
    ns = SEQ // ts
    nqk = RET_HEADS * RET_DK
    return pl.pallas_call(
        functools.partial(_ret_kernel, chunk=RET_CHUNK),
        out_shape=[jax.ShapeDtypeStruct((BATCH * SEQ, D_RET), BF16),
                   jax.ShapeDtypeStruct((BATCH, RET_HEADS, RET_DK, RET_DV), F32)],
        grid=(BATCH, ns),
        in_specs=[pl.BlockSpec((ts, nqk), lambda b, i: (b * ns + i, 0)),
                  pl.BlockSpec((ts, nqk), lambda b, i: (b * ns + i, 0)),
                  pl.BlockSpec((ts, D_RET), lambda b, i: (b * ns + i, 0)),
                  pl.BlockSpec((ts, D_RET), lambda b, i: (b * ns + i, 0))],
        out_specs=[pl.BlockSpec((ts, D_RET), lambda b, i: (b * ns + i, 0)),
                   pl.BlockSpec((None, RET_HEADS, RET_DK, RET_DV), lambda b, i: (b, 0, 0, 0))],
        compiler_params=_cparams(2),
        name="prompt_retention",
    )(rq, rk, rv, g_ret)


def _merge_parts(a_ref, r_ref, w_ref, x_ref, g_ref, y_ref, *h_ref):
    width = D_MODEL // N_COL_CHUNKS

    def part(c):
        cols = slice(c * width, (c + 1) * width)
        ar = jnp.concatenate([a_ref[...], r_ref[...]], axis=-1)
        y_ref[:, cols] = x_ref[:, cols] + jnp.dot(ar, w_ref[:, cols], preferred_element_type=F32)

    def norm_part():
        h_ref[0][...] = (_rms_rows(y_ref[...]) * g_ref[...]).astype(BF16)

    parts = [functools.partial(part, c) for c in range(N_COL_CHUNKS)]
    return parts + [norm_part] if h_ref else parts


def _merge(a, r, w_out, layer, x, g_next, tm, decode=None):
    t = x.shape[0]
    with_h = g_next is not None
    out_shape = [jax.ShapeDtypeStruct((t, D_MODEL), F32)]
    out_specs = [pl.BlockSpec((tm, D_MODEL), lambda m, *_: (m, 0))]
    if with_h:
        out_shape.append(jax.ShapeDtypeStruct((t, D_MODEL), BF16))
        out_specs.append(pl.BlockSpec((tm, D_MODEL), lambda m, *_: (m, 0)))
    else:
        g_next = jnp.ones((1, D_MODEL), F32)
    in_specs = [pl.BlockSpec((tm, D_DA), lambda m, *_: (m, 0)),
                pl.BlockSpec((tm, D_RET), lambda m, *_: (m, 0)),
                pl.BlockSpec((None, D_MODEL, D_MODEL), lambda m, *_: (layer, 0, 0)),
                pl.BlockSpec((tm, D_MODEL), lambda m, *_: (m, 0)),
                pl.BlockSpec((1, D_MODEL), lambda m, *_: (0, 0))]
    res = list(_launch_rows(_merge_parts, t // tm, [a, r, w_out, x, g_next], in_specs, out_shape, out_specs,
                            {}, "merge", decode))
    dec_row = res.pop() if decode is not None else None
    return res[0], (res[1] if with_h else None), dec_row


def _dec_attn_kernel(pt_ref, small_ref, *rest, lam_init, npg, step_axis, co_parts):
    del pt_ref
    q_ref, kn_ref, vn_ref, g_ref = (small_ref.at[i:i + 1, :] for i in range(4))
    lq1_ref, lk1_ref, lq2_ref, lk2_ref = (
        small_ref.at[4:5, i * DA_HEAD_DIM:(i + 1) * DA_HEAD_DIM] for i in range(4))
    subg_ref = small_ref.at[4:5, 4 * DA_HEAD_DIM:4 * DA_HEAD_DIM + DA_DV]
    k_refs = rest[:npg]
    v_refs = rest[npg:2 * npg]
    o_ref = rest[2 * npg]
    qm_ref, m_ref, l_ref, acc_ref = rest[2 * npg + 1:]
    p_idx = pl.program_id(step_axis)
    rows = SAMPLE_ROWS

    @pl.when(p_idx == 0)
    def _():
        r = lax.broadcasted_iota(jnp.int32, (rows, D_DA), 0)
        cgrp = lax.broadcasted_iota(jnp.int32, (rows, D_DA), 1) // DA_HEAD_DIM
        qb = jnp.broadcast_to(q_ref[...], (rows, D_DA))
        qm_ref[...] = jnp.where(r == cgrp, qb, 0.0).astype(BF16)
        m_ref[...] = jnp.full(m_ref.shape, -jnp.inf, F32)
        l_ref[...] = jnp.zeros(l_ref.shape, F32)
        acc_ref[...] = jnp.zeros(acc_ref.shape, F32)

    def score_part(ks, ntok_valid):
        qm = qm_ref[...]
        s = jnp.concatenate([_dot_nt(qm, k) for k in ks], axis=-1)
        if ntok_valid is not None:
            tcol = lax.broadcasted_iota(jnp.int32, s.shape, 1)
            s = jnp.where(tcol < ntok_valid, s, -jnp.inf)
        m_prev = m_ref[...]
        m_new = jnp.maximum(m_prev, jnp.max(s, axis=-1, keepdims=True))
        alpha = jnp.exp2(m_prev - m_new)
        p = jnp.exp2(s - m_new)
        l_ref[...] = alpha * l_ref[...] + jnp.sum(p, axis=-1, keepdims=True)
        m_ref[...] = m_new
        return alpha, p.astype(BF16)

    def value_part(alpha, pb, vs):
        ntok = pb.shape[1] // len(vs)
        pv = jnp.dot(pb[:, :ntok], vs[0], preferred_element_type=F32)
        for i in range(1, len(vs)):
            pv = pv + jnp.dot(pb[:, i * ntok:(i + 1) * ntok], vs[i], preferred_element_type=F32)
        acc_ref[...] = alpha * acc_ref[...] + pv

    def load_k(r):
        n_hc = 2 * DA_HEADS
        return jnp.concatenate(
            [r[pl.ds(hc, PAGE_SIZE, stride=n_hc), :].astype(BF16) for hc in range(n_hc)], axis=-1)

    def load_v(r):
        n_half = DA_DV // LANES
        return jnp.concatenate(
            [r[pl.ds(j * DA_HEADS + h, PAGE_SIZE, stride=n_half * DA_HEADS), :].astype(BF16)
             for h in range(DA_HEADS) for j in range(n_half)], axis=-1)

    assert npg % DEC_PAGE_GROUPS == 0
    per = npg // DEC_PAGE_GROUPS
    carried = {}
    dec_parts = []
    for gi in range(DEC_PAGE_GROUPS):
        grp = slice(gi * per, (gi + 1) * per)

        def score(grp=grp):
            carried["alpha"], carried["p"] = score_part([load_k(r) for r in k_refs[grp]], None)

        def value(grp=grp):
            value_part(carried["alpha"], carried["p"], [load_v(r) for r in v_refs[grp]])

        dec_parts += [score, value]
    co_parts = list(co_parts)
    for i in range(max(len(co_parts), len(dec_parts))):
        if i < len(co_parts):
            co_parts[i]()
        if i < len(dec_parts):
            dec_parts[i]()

    @pl.when(p_idx == pl.num_programs(step_axis) - 1)
    def _():
        r = lax.broadcasted_iota(jnp.int32, (rows, D_DA), 0)
        kn = jnp.where(r == 0, jnp.broadcast_to(kn_ref[...], (rows, D_DA)), 0.0).astype(BF16)
        vn = jnp.where(r == 0, jnp.broadcast_to(vn_ref[...], (rows, D_DA)), 0.0).astype(BF16)
        value_part(*score_part([kn], 1), [vn])
        lam = _lam_value(lq1_ref[...], lk1_ref[...], lq2_ref[...], lk2_ref[...], lam_init)
        o = acc_ref[...] / l_ref[...]
        g = g_ref[...]
        subg = subg_ref[...]
        for h in range(DA_HEADS):
            sl = slice(h * DA_DV, (h + 1) * DA_DV)
            att = o[2 * h:2 * h + 1, sl] - lam * o[2 * h + 1:2 * h + 2, sl]
            o_ref[:, sl] = _subln_gate(att, subg, g[:, sl], lam_init)


class _DecodeJob(NamedTuple):
    page_table: jax.Array
    cache_k: jax.Array
    cache_v: jax.Array
    small: jax.Array
    lam_init: float
    layer: int


DEC_SMALL_ROWS = 8


def _pack_decode_rows(q, k_new, v_new, g_da, lam_vecs, subg):
    shared = jnp.concatenate([*lam_vecs, subg], axis=-1)
    shared = jnp.pad(shared, ((0, 0), (0, D_DA - shared.shape[1])))
    rows = [q, k_new, v_new, g_da, jnp.broadcast_to(shared, (DEC_BATCH, D_DA))]
    rows += [jnp.zeros((DEC_BATCH, D_DA), F32)] * (DEC_SMALL_ROWS - len(rows))
    return jnp.stack(rows, axis=1)


class _DecodeOperands(NamedTuple):
    inputs: list
    in_specs: list
    out_shape: jax.ShapeDtypeStruct
    out_spec: pl.BlockSpec
    scratch_shapes: list


def _decode_operands(job, npg, first, n_seq, seq_and_step):
    small = pl.BlockSpec((None, DEC_SMALL_ROWS, D_DA), lambda *a: (seq_and_step(a[:-1])[0], 0, 0))

    def page_spec(i):
        def index(*a):
            b, step = seq_and_step(a[:-1])
            return (job.layer * N_POOL + a[-1][b, step * npg + i], 0, 0)
        return pl.BlockSpec((None, PAGE_SIZE * 2 * DA_HEADS, LANES), index)

    pages = [page_spec(i) for i in range(npg)]
    return _DecodeOperands(
        inputs=[job.small, *([job.cache_k] * npg), *([job.cache_v] * npg)],
        in_specs=[small] + pages + pages,
        out_shape=jax.ShapeDtypeStruct((n_seq, 1, D_DA), F32),
        out_spec=pl.BlockSpec((None, 1, D_DA), lambda *a: (seq_and_step(a[:-1])[0] - first, 0, 0)),
        scratch_shapes=[pltpu.VMEM((SAMPLE_ROWS, D_DA), BF16),
                        pltpu.VMEM((SAMPLE_ROWS, 1), F32), pltpu.VMEM((SAMPLE_ROWS, 1), F32),
                        pltpu.VMEM((SAMPLE_ROWS, D_DA), F32)])


def _dec_ret_kernel(st_ref, qc_ref, kc_ref, v_ref, g_ref, nst_ref, o_ref):
    v_all = v_ref[...]
    g_all = g_ref[...]
    for h in range(RET_HEADS):
        dec = float(1.0 - 2.0 ** (-5.0 - h))
        sl = slice(h * RET_DV, (h + 1) * RET_DV)
        st = st_ref[h]
        qc = qc_ref[h]
        kc = kc_ref[h]
        v = v_all[:, sl]
        nst_ref[h] = st * dec + kc * v
        cross = jnp.sum(qc * st, axis=0, keepdims=True) * dec
        intra = jnp.sum(qc * kc, axis=0, keepdims=True) * v
        o_ref[:, sl] = _rms_rows(intra + cross) * g_all[:, sl]


def _decode_retention(state, rq_col, rk_col, rv, g_ret, layer):
    col = pl.BlockSpec((None, RET_HEADS, RET_DK, 1), lambda b: (b, 0, 0, 0))
    row = pl.BlockSpec((None, 1, D_RET), lambda b: (b, 0, 0))
    return pl.pallas_call(
        _dec_ret_kernel,
        out_shape=[jax.ShapeDtypeStruct((DEC_BATCH, RET_HEADS, RET_DK, RET_DV), F32),
                   jax.ShapeDtypeStruct((DEC_BATCH, 1, D_RET), F32)],
        grid=(DEC_BATCH,),
        in_specs=[pl.BlockSpec((None, None, RET_HEADS, RET_DK, RET_DV), lambda b: (layer, b, 0, 0, 0)),
                  col, col, row, row],
        out_specs=[pl.BlockSpec((None, RET_HEADS, RET_DK, RET_DV), lambda b: (b, 0, 0, 0)),
                   pl.BlockSpec((None, 1, D_RET), lambda b: (b, 0, 0))],
        compiler_params=_cparams(1),
        name="decode_retention",
    )(state, rq_col, rk_col, rv, g_ret)


def _rotary_tables(pos):
    half = RET_DK // 2
    theta = 1.0 / (ROPE_BASE ** jnp.linspace(0.0, 1.0, half, dtype=F32))
    ang = pos.astype(F32)[:, None] * theta[None, :]
    cos, sin = jnp.cos(ang), jnp.sin(ang)
    return jnp.concatenate([cos, cos], axis=-1), jnp.concatenate([-sin, sin], axis=-1)


def _pad_rows(x):
    return jnp.pad(x, ((0, SAMPLE_ROWS - x.shape[0]), (0, 0)))


def kernel(x_prompt, x_sample, cache_k, cache_v, state_ret, page_table, norm_g, w_in, w_out,
           qn_g, kn_g, lam_q1, lam_k1, lam_q2, lam_k2, subln_g):
    t_p = BATCH * SEQ
    w_in_bf = w_in[0:1].astype(BF16)
    w_out_bf = w_out.astype(BF16)
    cos_p, sin_p = _rotary_tables(jnp.arange(SEQ))
    cos_s, sin_s = _rotary_tables(jnp.full((SAMPLE_ROWS,), PAST_LEN))
    ck = cache_k.reshape(DEPTH * N_POOL, PAGE_SIZE * DA_HEADS * 2, DA_HEAD_DIM)
    cv = _values_to_cache_order(cache_v)

    xp = x_prompt.reshape(t_p, D_MODEL)
    xs = _pad_rows(x_sample.reshape(DEC_BATCH, D_MODEL))
    hp = _input_norm(xp, norm_g[0][None], PROMPT_TM)
    hs = _input_norm(xs, norm_g[0][None], SAMPLE_ROWS)

    kp_all = vp_all = None
    sp_l, ks_l, vs_l, ss_l = [], [], [], []
    for l in range(DEPTH):
        lam_init = 0.8 - 0.6 * math.exp(-0.3 * l)
        lam_vecs = [a[l][None] for a in (lam_q1, lam_k1, lam_q2, lam_k2)]
        subg = subln_g[l][None]
        g_next = norm_g[l + 1][None] if l + 1 < DEPTH else None

        zs = _sample_project(hs, w_in_bf, qn_g[l][None], kn_g[l][None], cos_s, sin_s)
        seg = lambda col, width: zs[:, col:col + width]
        qs, k_row, v_row, g_da_s = (seg(c, D_DA) for c in (COL_Q, COL_K, COL_V, COL_GDA))
        rq_s, rk_s = seg(COL_RQ, RET_HEADS * RET_DK), seg(COL_RK, RET_HEADS * RET_DK)
        rv_s, g_ret_s = seg(COL_RV, D_RET), seg(COL_GRET, D_RET)
        as_row = lambda z: z[:DEC_BATCH].reshape(DEC_BATCH, 1, z.shape[-1])
        as_col = lambda z: z[:DEC_BATCH].reshape(DEC_BATCH, RET_HEADS, RET_DK, 1)
        small = _pack_decode_rows(qs[:DEC_BATCH], k_row[:DEC_BATCH], v_row[:DEC_BATCH], g_da_s[:DEC_BATCH],
                                  lam_vecs, subg)
        job = _DecodeJob(page_table, ck, cv, small, lam_init, l)

        q, kp_all, kbf, vp_all, vbf, g_da, rq, rk, rv, g_ret, dec_rows = _project(
            hp, w_in_bf, l, qn_g[l][None], kn_g[l][None], cos_p, sin_p, PROMPT_TM, BF16,
            kv_stacks=(kp_all, vp_all), decode_job=job)
        a, w_in_bf = _prompt_attention(q, kbf, vbf, g_da, lam_vecs, subg, lam_init,
                                       w_next=(w_in, l + 1) if l + 1 < DEPTH else None)
        r, st = _prompt_retention(rq, rk, rv, g_ret)
        xp, hp, dec_row = _merge(a, r, w_out_bf, l, xp, g_next, MERGE_TM, decode=(job, len(dec_rows)))
        dec_rows.append(dec_row)
        sp_l.append(st)

        assert len(dec_rows) == DEC_BATCH
        a = jnp.concatenate(dec_rows, axis=0)
        rq, rk, rv, g_ret = rq_s, rk_s, rv_s, g_ret_s
        nst, r = _decode_retention(state_ret, as_col(rq), as_col(rk), as_row(rv), as_row(g_ret), l)
        a16 = _pad_rows(a.reshape(DEC_BATCH, D_DA)).astype(BF16)
        r16 = _pad_rows(r.reshape(DEC_BATCH, D_RET)).astype(BF16)
        xs, hs, _ = _merge(a16, r16, w_out_bf, l, xs, g_next, SAMPLE_ROWS)
        ks_l.append(k_row[:DEC_BATCH].reshape(DEC_BATCH, DEC_SEQ, DA_HEADS, 2, DA_HEAD_DIM))
        vs_l.append(v_row[:DEC_BATCH].reshape(DEC_BATCH, DEC_SEQ, DA_HEADS, DA_DV))
        ss_l.append(nst)

    return (xp.reshape(BATCH, SEQ, D_MODEL),
            xs[:DEC_BATCH].reshape(DEC_BATCH, DEC_SEQ, D_MODEL),
            _keys_from_cache_order(kp_all, (DEPTH, BATCH, SEQ)),
            _values_from_cache_order(vp_all, (DEPTH, BATCH, SEQ)), jnp.stack(sp_l),
            jnp.stack(ks_l), jnp.stack(vs_l), jnp.stack(ss_l))
```

```python
import functools
import math
from typing import NamedTuple

import numpy as np
import jax
import jax.numpy as jnp
from jax import lax
from jax.experimental import pallas as pl
from jax.experimental.pallas import tpu as pltpu

D_MODEL = 2048
BATCH = 4
SEQ = 2048
DEPTH = 4
DEC_BATCH = 8
DEC_SEQ = 1
PAST_LEN = 16384
PAGE_SIZE = 128
N_PAGES = PAST_LEN // PAGE_SIZE
N_POOL = (DEC_BATCH * N_PAGES * 5) // 4

D_DA = D_MODEL // 2
D_RET = D_MODEL - D_DA
DA_HEAD_DIM = 128
DA_HEADS = D_DA // (2 * DA_HEAD_DIM)
DA_DV = 2 * DA_HEAD_DIM
RET_HEADS = 4
RET_DV = D_RET // RET_HEADS
RET_DK = RET_DV // 2
ROPE_BASE = 10000.0
EPS = 1e-6
D_IN = 4 * D_DA + 2 * RET_HEADS * RET_DK + 2 * D_RET

COL_Q, COL_K, COL_V, COL_GDA = 0, D_DA, 2 * D_DA, 3 * D_DA
COL_RQ = 4 * D_DA
COL_RK = COL_RQ + RET_HEADS * RET_DK
COL_RV = COL_RK + RET_HEADS * RET_DK
COL_GRET = COL_RV + D_RET

LANES = 128
SAMPLE_ROWS = 16
VMEM_LIMIT = 48 * 1024 * 1024

PROMPT_TM = 512
MERGE_TM = 256
N_COL_CHUNKS = 4
DEC_PAGE_GROUPS = 2
ATTN_TQ = 512
ATTN_HEADS_PER_STEP = 4
Q_SCALE = DA_HEAD_DIM ** -0.5 * math.log2(math.e)
RET_CHUNK = 128
RET_TILE = 1024

LOG_DECAY = np.log(1.0 - 2.0 ** (-5.0 - np.arange(RET_HEADS, dtype=np.float32))).astype(np.float32)

F32 = jnp.float32
BF16 = jnp.bfloat16


def _cparams(n_axes):
    return pltpu.CompilerParams(dimension_semantics=("arbitrary",) * n_axes,
                                vmem_limit_bytes=VMEM_LIMIT)


def _silu(g):
    return g / (1.0 + jnp.exp(-g))


def _rms_rows(x):
    return x * lax.rsqrt(jnp.mean(x * x, axis=-1, keepdims=True) + EPS)


def _dot_nt(a, b):
    return lax.dot_general(a, b, (((1,), (1,)), ((), ())), preferred_element_type=F32)


def _dot_tn(a, b):
    return lax.dot_general(a, b, (((0,), (0,)), ((), ())), preferred_element_type=F32)


def _norm_kernel(x_ref, g_ref, h_ref):
    h_ref[...] = (_rms_rows(x_ref[...]) * g_ref[...]).astype(BF16)


def _input_norm(x, g, tm):
    t = x.shape[0]
    return pl.pallas_call(
        _norm_kernel,
        out_shape=jax.ShapeDtypeStruct((t, D_MODEL), BF16),
        grid=(t // tm,),
        in_specs=[pl.BlockSpec((tm, D_MODEL), lambda m: (m, 0)),
                  pl.BlockSpec((1, D_MODEL), lambda m: (0, 0))],
        out_specs=pl.BlockSpec((tm, D_MODEL), lambda m: (m, 0)),
        compiler_params=_cparams(1),
        name="input_norm",
    )(x, g)


def _chunk_dots(h_ref, w_ref, n_chunks, epilogue):
    width = w_ref.shape[1] // n_chunks

    def part(c):
        z = jnp.dot(h_ref[...], w_ref[:, c * width:(c + 1) * width], preferred_element_type=F32)
        epilogue(c, z)

    return [functools.partial(part, c) for c in range(n_chunks)]


def _proj_qk_parts(h_ref, w_ref, g_ref, *out_refs, scale):
    n_grp = D_DA // DA_HEAD_DIM
    grp_per_chunk = n_grp // N_COL_CHUNKS

    def epilogue(c, z):
        tm = z.shape[0]
        g = g_ref[...]
        for jj in range(grp_per_chunk):
            j = c * grp_per_chunk + jj
            y = _rms_rows(z[:, jj * DA_HEAD_DIM:(jj + 1) * DA_HEAD_DIM]) * g
            if scale is not None:
                y = y * scale
            for o in out_refs:
                if o.shape[1] == DA_HEAD_DIM:
                    o[pl.ds(j, tm, stride=n_grp), :] = y.astype(o.dtype)
                else:
                    o[:, j * DA_HEAD_DIM:(j + 1) * DA_HEAD_DIM] = y.astype(o.dtype)

    return _chunk_dots(h_ref, w_ref, N_COL_CHUNKS, epilogue)


def _proj_v_parts(h_ref, w_ref, v_ref, vact_ref):
    assert N_COL_CHUNKS == DA_HEADS

    def epilogue(h, z):
        tm = z.shape[0]
        for j in range(DA_DV // LANES):
            v_ref[pl.ds(j * DA_HEADS + h, tm, stride=2 * DA_HEADS), :] = z[:, j * LANES:(j + 1) * LANES]
        vact_ref[:, h * DA_DV:(h + 1) * DA_DV] = z.astype(vact_ref.dtype)

    return _chunk_dots(h_ref, w_ref, N_COL_CHUNKS, epilogue)


def _proj_copy_parts(h_ref, w_ref, *out_refs):
    def epilogue(c, z):
        width = z.shape[1]
        for o in out_refs:
            o[:, c * width:(c + 1) * width] = z.astype(o.dtype)

    return _chunk_dots(h_ref, w_ref, N_COL_CHUNKS, epilogue)


def _proj_gate_parts(h_ref, w_ref, o_ref):
    def epilogue(c, z):
        width = z.shape[1]
        o_ref[:, c * width:(c + 1) * width] = _silu(z).astype(o_ref.dtype)

    return _chunk_dots(h_ref, w_ref, N_COL_CHUNKS, epilogue)


def _proj_rot_parts(h_ref, w_ref, cos_ref, sin_ref, rq_ref, rk_ref):
    grp_per_chunk = 2 * RET_HEADS // N_COL_CHUNKS

    def epilogue(c, z):
        cos2 = cos_ref[...]
        sin2 = sin_ref[...]
        for jj in range(grp_per_chunk):
            j = c * grp_per_chunk + jj
            x = z[:, jj * RET_DK:(jj + 1) * RET_DK]
            if j >= RET_HEADS:
                x = x * (RET_DK ** -0.5)
            y = x * cos2 + pltpu.roll(x, RET_DK // 2, axis=1) * sin2
            if j < RET_HEADS:
                rq_ref[:, j * RET_DK:(j + 1) * RET_DK] = y.astype(rq_ref.dtype)
            else:
                jk = j - RET_HEADS
                rk_ref[:, jk * RET_DK:(jk + 1) * RET_DK] = y.astype(rk_ref.dtype)

    return _chunk_dots(h_ref, w_ref, N_COL_CHUNKS, epilogue)


def _run_parts(parts_fn):
    def kern(*refs):
        for part in parts_fn(*refs):
            part()
    return kern


def _with_carried_buffer(parts_fn):
    def wrapped(carried_ref, *refs):
        del carried_ref
        return parts_fn(*refs)
    return wrapped


def _with_decode(parts_fn, dec_kern, n_tile_in, n_dec_in, n_tile_out):
    def wrapped(pt_ref, *refs):
        tile_in = refs[:n_tile_in]
        dec_in = refs[n_tile_in:n_tile_in + n_dec_in]
        outs = refs[n_tile_in + n_dec_in:]
        dec_kern(pt_ref, *dec_in, *outs[n_tile_out:], co_parts=parts_fn(*tile_in, *outs[:n_tile_out]))
    return wrapped


def _proj_call(kern, h, w_in, layer, col0, ncols, tm, extra, extra_specs, outs, name, stacked=None,
               decode=None):
    t = h.shape[0]
    nm = t // tm
    assert col0 % ncols == 0
    cb = col0 // ncols
    out_shape = [jax.ShapeDtypeStruct((t * r, w), dt) for (r, w, dt) in outs]
    out_specs = [pl.BlockSpec((tm * r, w), lambda m, *_: (m, 0)) for (r, w, _) in outs]
    inputs = [h, w_in, *extra]
    in_specs = [pl.BlockSpec((tm, D_MODEL), lambda m, *_: (m, 0)),
                pl.BlockSpec((None, D_MODEL, ncols), lambda m, *_: (0, 0, cb))] + extra_specs
    aliases = {}
    if stacked is not None:
        i, buf = stacked
        r, w, dt = outs[i]
        out_shape[i] = jax.ShapeDtypeStruct((DEPTH * t * r, w), dt)
        out_specs[i] = pl.BlockSpec((tm * r, w), lambda m, *_: (layer * nm + m, 0))
        if buf is not None:
            kern = _with_carried_buffer(kern)
            inputs = [buf] + inputs
            in_specs = [pl.BlockSpec(memory_space=pl.ANY)] + in_specs
            aliases = {0: i}
    return _launch_rows(kern, nm, inputs, in_specs, out_shape, out_specs, aliases, name, decode)


def _launch_rows(kern, nm, inputs, in_specs, out_shape, out_specs, aliases, name, decode):
    if decode is None:
        return pl.pallas_call(
            _run_parts(kern),
            out_shape=out_shape,
            grid=(nm,),
            in_specs=in_specs,
            out_specs=out_specs,
            input_output_aliases=aliases,
            compiler_params=_cparams(1),
            name=name,
        )(*inputs)

    job, b = decode
    npg = N_PAGES // nm
    assert npg * nm == N_PAGES
    dec = _decode_operands(job, npg, b, 1, lambda idx: (b, idx[0]))
    body = _with_decode(kern, functools.partial(_dec_attn_kernel, lam_init=job.lam_init, npg=npg, step_axis=0),
                        len(inputs), len(dec.inputs), len(out_shape))
    grid_spec = pltpu.PrefetchScalarGridSpec(
        num_scalar_prefetch=1,
        grid=(nm,),
        in_specs=in_specs + dec.in_specs,
        out_specs=out_specs + [dec.out_spec],
        scratch_shapes=dec.scratch_shapes,
    )
    return pl.pallas_call(
        body,
        out_shape=out_shape + [dec.out_shape],
        grid_spec=grid_spec,
        input_output_aliases={k + 1: v for k, v in aliases.items()},
        compiler_params=_cparams(1),
        name=name + "_dec",
    )(job.page_table, *inputs, *dec.inputs)


N_PROJ_CALLS = 7


def _project(h, w_in, layer, qn_g, kn_g, cos2, sin2, tm, kv_stacks, decode_job):
    act_dtype = BF16
    k_stacked = (0, kv_stacks[0])
    v_stacked = (0, kv_stacks[1])
    gspec = [pl.BlockSpec((1, DA_HEAD_DIM), lambda m, *_: (0, 0))]
    cache_rows = 2 * DA_HEADS
    dec_rows = []

    def call(kern, col0, ncols, extra, extra_specs, outs, name, stacked=None):
        res = _proj_call(kern, h, w_in, layer, col0, ncols, tm, extra, extra_specs, outs, name,
                         stacked=stacked, decode=(decode_job, len(dec_rows)))
        dec_rows.append(res[-1])
        return res[:-1]

    (q,) = call(functools.partial(_proj_qk_parts, scale=Q_SCALE), COL_Q, D_DA, [qn_g], gspec,
                [(1, D_DA, act_dtype)], "proj_q")
    k32, kact = call(functools.partial(_proj_qk_parts, scale=None), COL_K, D_DA, [kn_g], gspec,
                     [(cache_rows, DA_HEAD_DIM, F32), (1, D_DA, act_dtype)], "proj_k", stacked=k_stacked)
    v32, vact = call(_proj_v_parts, COL_V, D_DA, [], [],
                     [(cache_rows, LANES, F32), (1, D_DA, act_dtype)], "proj_v", stacked=v_stacked)
    (g_da,) = call(_proj_gate_parts, COL_GDA, D_DA, [], [], [(1, D_DA, act_dtype)], "proj_gda")
    n_tab = cos2.shape[0] // tm
    tspec = [pl.BlockSpec((tm, RET_DK), lambda m, *_: (m % n_tab, 0))] * 2
    nrot = 2 * RET_HEADS * RET_DK
    rq, rk = call(_proj_rot_parts, COL_RQ, nrot, [cos2, sin2], tspec,
                  [(1, nrot // 2, act_dtype), (1, nrot // 2, act_dtype)], "proj_rot")
    (rv,) = call(_proj_copy_parts, COL_RV, D_RET, [], [], [(1, D_RET, act_dtype)], "proj_rv")
    (g_ret,) = call(_proj_gate_parts, COL_GRET, D_RET, [], [], [(1, D_RET, act_dtype)], "proj_gret")
    assert len(dec_rows) == N_PROJ_CALLS
    return q, k32, kact, v32, vact, g_da, rq, rk, rv, g_ret, dec_rows


SAMPLE_SEG = D_DA


def _sample_proj_kernel(h_ref, w_ref, qg_ref, kg_ref, cos_ref, sin_ref, o_ref):
    j = pl.program_id(0)
    z = jnp.dot(h_ref[...], w_ref[...], preferred_element_type=F32)

    def norm_groups(g, scale):
        for grp in range(SAMPLE_SEG // DA_HEAD_DIM):
            sl = slice(grp * DA_HEAD_DIM, (grp + 1) * DA_HEAD_DIM)
            y = _rms_rows(z[:, sl]) * g
            o_ref[:, sl] = y * scale if scale is not None else y

    @pl.when(j == COL_Q // SAMPLE_SEG)
    def _():
        norm_groups(qg_ref[...], Q_SCALE)

    @pl.when(j == COL_K // SAMPLE_SEG)
    def _():
        norm_groups(kg_ref[...], None)

    @pl.when((j == COL_V // SAMPLE_SEG) | (j == COL_RV // SAMPLE_SEG))
    def _():
        o_ref[...] = z

    @pl.when((j == COL_GDA // SAMPLE_SEG) | (j == COL_GRET // SAMPLE_SEG))
    def _():
        o_ref[...] = _silu(z)

    @pl.when(j == COL_RQ // SAMPLE_SEG)
    def _():
        cos2 = cos_ref[...]
        sin2 = sin_ref[...]
        for grp in range(2 * RET_HEADS):
            sl = slice(grp * RET_DK, (grp + 1) * RET_DK)
            x = z[:, sl]
            if grp >= RET_HEADS:
                x = x * (RET_DK ** -0.5)
            o_ref[:, sl] = x * cos2 + pltpu.roll(x, RET_DK // 2, axis=1) * sin2


def _sample_project(h, w_in, qn_g, kn_g, cos2, sin2):
    assert 2 * RET_HEADS * RET_DK == SAMPLE_SEG and COL_RK == COL_RQ + RET_HEADS * RET_DK
    rows = h.shape[0]
    const = lambda shape: pl.BlockSpec(shape, lambda j: (0,) * len(shape))
    return pl.pallas_call(
        _sample_proj_kernel,
        out_shape=jax.ShapeDtypeStruct((rows, D_IN), F32),
        grid=(D_IN // SAMPLE_SEG,),
        in_specs=[const((rows, D_MODEL)),
                  pl.BlockSpec((None, D_MODEL, SAMPLE_SEG), lambda j: (0, 0, j)),
                  const((1, DA_HEAD_DIM)), const((1, DA_HEAD_DIM)),
                  const((rows, RET_DK)), const((rows, RET_DK))],
        out_specs=pl.BlockSpec((rows, SAMPLE_SEG), lambda j: (0, j)),
        compiler_params=_cparams(1),
        name="sample_proj",
    )(h, w_in, qn_g, kn_g, cos2, sin2)


def _keys_from_cache_order(k32, lead):
    return k32.reshape(lead + (DA_HEADS, 2, DA_HEAD_DIM))


def _values_from_cache_order(v32, lead):
    n_half = DA_DV // LANES
    v = v32.reshape(lead + (n_half, DA_HEADS, LANES))
    nd = len(lead)
    v = v.transpose(tuple(range(nd)) + (nd + 1, nd, nd + 2))
    return v.reshape(lead + (DA_HEADS, DA_DV))


def _values_to_cache_order(cache_v):
    n_half = DA_DV // LANES
    v = cache_v.reshape(-1, PAGE_SIZE, DA_HEADS, n_half, LANES).transpose(0, 1, 3, 2, 4)
    return v.reshape(-1, PAGE_SIZE * n_half * DA_HEADS, LANES)


def _lam_value(lq1, lk1, lq2, lk2, lam_init):
    s1 = jnp.sum(lq1 * lk1, axis=-1, keepdims=True)
    s2 = jnp.sum(lq2 * lk2, axis=-1, keepdims=True)
    return jnp.exp(s1) - jnp.exp(s2) + lam_init


def _subln_gate(o, subg, gate, lam_init):
    return (_rms_rows(o) * subg) * (1.0 - lam_init) * gate.astype(F32)


def _attn_with_weight_cast(lq1_ref, lk1_ref, lq2_ref, lk2_ref, subg_ref, q_ref, k_ref, v_ref, g_ref,
                           w32_ref, o_ref, w16_ref, m_ref, l_ref, acc_ref, **kw):
    w16_ref[...] = w32_ref[...].astype(w16_ref.dtype)
    _attn_kernel(lq1_ref, lk1_ref, lq2_ref, lk2_ref, subg_ref, q_ref, k_ref, v_ref, g_ref, o_ref,
                 m_ref, l_ref, acc_ref, **kw)


def _attn_kernel(lq1_ref, lk1_ref, lq2_ref, lk2_ref, subg_ref, q_ref, k_ref, v_ref, g_ref, o_ref,
                 m_ref, l_ref, acc_ref, *, lam_init, tq, hps):
    qi = pl.program_id(2)
    q = q_ref[...]
    d = DA_HEAD_DIM

    m_ref[...] = jnp.full(m_ref.shape, -jnp.inf, F32)
    l_ref[...] = jnp.zeros(l_ref.shape, F32)
    acc_ref[...] = jnp.zeros(acc_ref.shape, F32)

    def step(r0, nr, k, v, mask):
        rows = slice(r0, r0 + nr)
        n_lt = k.shape[0] // LANES
        upd = []
        for ci in range(2 * hps):
            hh = ci // 2
            s = _dot_nt(q[rows, ci * d:(ci + 1) * d], k[:, ci * d:(ci + 1) * d])
            if mask is not None:
                s = jnp.where(mask, s, -jnp.inf)
            st = [s[:, j * LANES:(j + 1) * LANES] for j in range(n_lt)]
            fold = st[0]
            for j in range(1, n_lt):
                fold = jnp.maximum(fold, st[j])
            m_prev = m_ref[ci, rows]
            m_new = jnp.maximum(m_prev, jnp.broadcast_to(jnp.max(fold, axis=-1, keepdims=True),
                                                         (nr, LANES)))
            alpha = jnp.exp2(m_prev - m_new)
            pt = [jnp.exp2(t - m_new) for t in st]
            psum = pt[0]
            for j in range(1, n_lt):
                psum = psum + pt[j]
            p = jnp.concatenate([t.astype(BF16) for t in pt], axis=-1)
            pv = jnp.dot(p, v[:, hh * DA_DV:(hh + 1) * DA_DV], preferred_element_type=F32)
            upd.append((m_new, alpha, psum, pv))
        for ci, (m_new, alpha, psum, pv) in enumerate(upd):
            m_ref[ci, rows] = m_new
            l_ref[ci, rows] = alpha * l_ref[ci, rows] + psum
            acc_ref[ci, rows] = jnp.concatenate([alpha] * (DA_DV // LANES), axis=-1) * acc_ref[ci, rows] + pv

    def body(ki, carry):
        off = pl.multiple_of(ki * tq, tq)
        step(0, tq, k_ref[pl.ds(off, tq), :], v_ref[pl.ds(off, tq), :], None)
        return carry

    lax.fori_loop(0, qi, body, 0)

    off = pl.multiple_of(qi * tq, tq)
    hq = tq // 2
    for r0, nk in ((0, hq), (hq, tq)):
        row = lax.broadcasted_iota(jnp.int32, (hq, nk), 0) + r0
        col = lax.broadcasted_iota(jnp.int32, (hq, nk), 1)
        step(r0, hq, k_ref[pl.ds(off, nk), :], v_ref[pl.ds(off, nk), :], col <= row)

    lam = _lam_value(lq1_ref[...], lk1_ref[...], lq2_ref[...], lk2_ref[...], lam_init)
    for hh in range(hps):
        cols = slice(hh * DA_DV, (hh + 1) * DA_DV)
        l1 = jnp.sum(l_ref[2 * hh], axis=-1, keepdims=True)
        l2 = jnp.sum(l_ref[2 * hh + 1], axis=-1, keepdims=True)
        o = acc_ref[2 * hh] / l1 - lam * (acc_ref[2 * hh + 1] / l2)
        o_ref[:, cols] = _subln_gate(o, subg_ref[...], g_ref[:, cols], lam_init).astype(o_ref.dtype)


def _prompt_attention(q, k, v, g_da, lam_vecs, subg, lam_init, w_next=None):
    tq = ATTN_TQ
    nq = SEQ // tq
    hps = ATTN_HEADS_PER_STEP
    nh = DA_HEADS // hps
    width = hps * DA_DV
    vec = pl.BlockSpec((1, DA_HEAD_DIM), lambda b, h, i: (0, 0))
    kern = _attn_kernel
    inputs = [*lam_vecs, subg, q, k, v, g_da]
    in_specs = [vec, vec, vec, vec,
                pl.BlockSpec((1, DA_DV), lambda b, h, i: (0, 0)),
                pl.BlockSpec((tq, width), lambda b, h, i: (b * nq + i, h)),
                pl.BlockSpec((SEQ, width), lambda b, h, i: (b, h)),
                pl.BlockSpec((SEQ, width), lambda b, h, i: (b, h)),
                pl.BlockSpec((tq, width), lambda b, h, i: (b * nq + i, h))]
    out_shape = [jax.ShapeDtypeStruct((BATCH * SEQ, D_DA), BF16)]
    out_specs = [pl.BlockSpec((tq, width), lambda b, h, i: (b * nq + i, h))]
    if w_next is not None:
        w_in, layer = w_next
        n_steps = BATCH * nh * nq
        rows = D_MODEL // n_steps
        assert rows * n_steps == D_MODEL
        kern = _attn_with_weight_cast
        inputs.append(w_in)
        in_specs.append(pl.BlockSpec((None, rows, D_IN), lambda b, h, i: (layer, (b * nh + h) * nq + i, 0)))
        out_shape.append(jax.ShapeDtypeStruct((1, D_MODEL, D_IN), BF16))
        out_specs.append(pl.BlockSpec((None, rows, D_IN), lambda b, h, i: (0, (b * nh + h) * nq + i, 0)))
    res = pl.pallas_call(
        functools.partial(kern, lam_init=lam_init, tq=tq, hps=hps),
        out_shape=out_shape,
        grid=(BATCH, nh, nq),
        in_specs=in_specs,
        out_specs=out_specs,
        scratch_shapes=[pltpu.VMEM((2 * hps, tq, LANES), F32), pltpu.VMEM((2 * hps, tq, LANES), F32),
                        pltpu.VMEM((2 * hps, tq, DA_DV), F32)],
        compiler_params=_cparams(3),
        name="prompt_attention",
    )(*inputs)
    return (res[0], res[1]) if w_next is not None else (res[0], None)


def _ret_kernel(q_ref, k_ref, v_ref, g_ref, o_ref, st_ref, *, chunk):
    c = chunk
    n_chunks = q_ref.shape[0] // c

    @pl.when(pl.program_id(1) == 0)
    def _():
        st_ref[...] = jnp.zeros(st_ref.shape, F32)

    li = lax.broadcasted_iota(jnp.int32, (c, 1), 0).astype(F32)
    diff = (lax.broadcasted_iota(jnp.int32, (c, c), 0)
            - lax.broadcasted_iota(jnp.int32, (c, c), 1)).astype(F32)
    decays = []
    for h in range(RET_HEADS):
        lg = float(LOG_DECAY[h])
        decays.append((jnp.where(diff >= 0, jnp.exp(lg * jnp.maximum(diff, 0.0)), 0.0),
                       jnp.exp((li + 1.0) * lg), jnp.exp((c - 1.0 - li) * lg), math.exp(c * lg)))

    def body(i, carry):
        off = pl.multiple_of(i * c, c)
        for h in range(RET_HEADS):
            dmask, cross_dec, k_dec, st_dec = decays[h]
            q = q_ref[pl.ds(off, c), h * RET_DK:(h + 1) * RET_DK]
            k = k_ref[pl.ds(off, c), h * RET_DK:(h + 1) * RET_DK]
            v = v_ref[pl.ds(off, c), h * RET_DV:(h + 1) * RET_DV]
            state = st_ref[h]
            s = _dot_nt(q, k) * dmask
            intra = jnp.dot(s.astype(BF16), v, preferred_element_type=F32)
            cross = jnp.dot(q, state.astype(BF16), preferred_element_type=F32) * cross_dec
            kd = (k.astype(F32) * k_dec).astype(BF16)
            st_ref[h] = state * st_dec + _dot_tn(kd, v)
            gate = g_ref[pl.ds(off, c), h * RET_DV:(h + 1) * RET_DV].astype(F32)
            o_ref[pl.ds(off, c), h * RET_DV:(h + 1) * RET_DV] = (
                _rms_rows(intra + cross) * gate).astype(o_ref.dtype)
        return carry

    lax.fori_loop(0, n_chunks, body, 0)


def _prompt_retention(rq, rk, rv, g_ret):
    ts = RET_TILE
    ns = SEQ // ts
    nqk = RET_HEADS * RET_DK
    return pl.pallas_call(
        functools.partial(_ret_kernel, chunk=RET_CHUNK),
        out_shape=[jax.ShapeDtypeStruct((BATCH * SEQ, D_RET), BF16),
                   jax.ShapeDtypeStruct((BATCH, RET_HEADS, RET_DK, RET_DV), F32)],
        grid=(BATCH, ns),
        in_specs=[pl.BlockSpec((ts, nqk), lambda b, i: (b * ns + i, 0)),
                  pl.BlockSpec((ts, nqk), lambda b, i: (b * ns + i, 0)),
                  pl.BlockSpec((ts, D_RET), lambda b, i: (b * ns + i, 0)),
                  pl.BlockSpec((ts, D_RET), lambda b, i: (b * ns + i, 0))],
        out_specs=[pl.BlockSpec((ts, D_RET), lambda b, i: (b * ns + i, 0)),
                   pl.BlockSpec((None, RET_HEADS, RET_DK, RET_DV), lambda b, i: (b, 0, 0, 0))],
        compiler_params=_cparams(2),
        name="prompt_retention",
    )(rq, rk, rv, g_ret)


def _merge_parts(a_ref, r_ref, w_ref, x_ref, g_ref, y_ref, *h_ref):
    width = D_MODEL // N_COL_CHUNKS

    def part(c):
        cols = slice(c * width, (c + 1) * width)
        ar = jnp.concatenate([a_ref[...], r_ref[...]], axis=-1)
        y_ref[:, cols] = x_ref[:, cols] + jnp.dot(ar, w_ref[:, cols], preferred_element_type=F32)

    def norm_part():
        h_ref[0][...] = (_rms_rows(y_ref[...]) * g_ref[...]).astype(BF16)

    parts = [functools.partial(part, c) for c in range(N_COL_CHUNKS)]
    return parts + [norm_part] if h_ref else parts


def _merge(a, r, w_out, layer, x, g_next, tm, decode=None):
    t = x.shape[0]
    with_h = g_next is not None
    out_shape = [jax.ShapeDtypeStruct((t, D_MODEL), F32)]
    out_specs = [pl.BlockSpec((tm, D_MODEL), lambda m, *_: (m, 0))]
    if with_h:
        out_shape.append(jax.ShapeDtypeStruct((t, D_MODEL), BF16))
        out_specs.append(pl.BlockSpec((tm, D_MODEL), lambda m, *_: (m, 0)))
    else:
        g_next = jnp.ones((1, D_MODEL), F32)
    in_specs = [pl.BlockSpec((tm, D_DA), lambda m, *_: (m, 0)),
                pl.BlockSpec((tm, D_RET), lambda m, *_: (m, 0)),
                pl.BlockSpec((None, D_MODEL, D_MODEL), lambda m, *_: (layer, 0, 0)),
                pl.BlockSpec((tm, D_MODEL), lambda m, *_: (m, 0)),
                pl.BlockSpec((1, D_MODEL), lambda m, *_: (0, 0))]
    res = list(_launch_rows(_merge_parts, t // tm, [a, r, w_out, x, g_next], in_specs, out_shape, out_specs,
                            {}, "merge", decode))
    dec_row = res.pop() if decode is not None else None
    return res[0], (res[1] if with_h else None), dec_row


def _dec_attn_kernel(pt_ref, small_ref, *rest, lam_init, npg, step_axis, co_parts):
    del pt_ref
    q_ref, kn_ref, vn_ref, g_ref = (small_ref.at[i:i + 1, :] for i in range(4))
    lq1_ref, lk1_ref, lq2_ref, lk2_ref = (
        small_ref.at[4:5, i * DA_HEAD_DIM:(i + 1) * DA_HEAD_DIM] for i in range(4))
    subg_ref = small_ref.at[4:5, 4 * DA_HEAD_DIM:4 * DA_HEAD_DIM + DA_DV]
    k_refs = rest[:npg]
    v_refs = rest[npg:2 * npg]
    o_ref = rest[2 * npg]
    qm_ref, m_ref, l_ref, acc_ref = rest[2 * npg + 1:]
    p_idx = pl.program_id(step_axis)
    rows = SAMPLE_ROWS

    @pl.when(p_idx == 0)
    def _():
        r = lax.broadcasted_iota(jnp.int32, (rows, D_DA), 0)
        cgrp = lax.broadcasted_iota(jnp.int32, (rows, D_DA), 1) // DA_HEAD_DIM
        qb = jnp.broadcast_to(q_ref[...], (rows, D_DA))
        qm_ref[...] = jnp.where(r == cgrp, qb, 0.0).astype(BF16)
        m_ref[...] = jnp.full(m_ref.shape, -jnp.inf, F32)
        l_ref[...] = jnp.zeros(l_ref.shape, F32)
        acc_ref[...] = jnp.zeros(acc_ref.shape, F32)

    def score_part(ks, ntok_valid):
        qm = qm_ref[...]
        s = jnp.concatenate([_dot_nt(qm, k) for k in ks], axis=-1)
        if ntok_valid is not None:
            tcol = lax.broadcasted_iota(jnp.int32, s.shape, 1)
            s = jnp.where(tcol < ntok_valid, s, -jnp.inf)
        m_prev = m_ref[...]
        m_new = jnp.maximum(m_prev, jnp.max(s, axis=-1, keepdims=True))
        alpha = jnp.exp2(m_prev - m_new)
        p = jnp.exp2(s - m_new)
        l_ref[...] = alpha * l_ref[...] + jnp.sum(p, axis=-1, keepdims=True)
        m_ref[...] = m_new
        return alpha, p.astype(BF16)

    def value_part(alpha, pb, vs):
        ntok = pb.shape[1] // len(vs)
        pv = jnp.dot(pb[:, :ntok], vs[0], preferred_element_type=F32)
        for i in range(1, len(vs)):
            pv = pv + jnp.dot(pb[:, i * ntok:(i + 1) * ntok], vs[i], preferred_element_type=F32)
        acc_ref[...] = alpha * acc_ref[...] + pv

    def load_k(r):
        n_hc = 2 * DA_HEADS
        return jnp.concatenate(
            [r[pl.ds(hc, PAGE_SIZE, stride=n_hc), :].astype(BF16) for hc in range(n_hc)], axis=-1)

    def load_v(r):
        n_half = DA_DV // LANES
        return jnp.concatenate(
            [r[pl.ds(j * DA_HEADS + h, PAGE_SIZE, stride=n_half * DA_HEADS), :].astype(BF16)
             for h in range(DA_HEADS) for j in range(n_half)], axis=-1)

    assert npg % DEC_PAGE_GROUPS == 0
    per = npg // DEC_PAGE_GROUPS
    carried = {}
    dec_parts = []
    for gi in range(DEC_PAGE_GROUPS):
        grp = slice(gi * per, (gi + 1) * per)

        def score(grp=grp):
            carried["alpha"], carried["p"] = score_part([load_k(r) for r in k_refs[grp]], None)

        def value(grp=grp):
            value_part(carried["alpha"], carried["p"], [load_v(r) for r in v_refs[grp]])

        dec_parts += [score, value]
    co_parts = list(co_parts)
    for i in range(max(len(co_parts), len(dec_parts))):
        if i < len(co_parts):
            co_parts[i]()
        if i < len(dec_parts):
            dec_parts[i]()

    @pl.when(p_idx == pl.num_programs(step_axis) - 1)
    def _():
        r = lax.broadcasted_iota(jnp.int32, (rows, D_DA), 0)
        kn = jnp.where(r == 0, jnp.broadcast_to(kn_ref[...], (rows, D_DA)), 0.0).astype(BF16)
        vn = jnp.where(r == 0, jnp.broadcast_to(vn_ref[...], (rows, D_DA)), 0.0).astype(BF16)
        value_part(*score_part([kn], 1), [vn])
        lam = _lam_value(lq1_ref[...], lk1_ref[...], lq2_ref[...], lk2_ref[...], lam_init)
        o = acc_ref[...] / l_ref[...]
        g = g_ref[...]
        subg = subg_ref[...]
        for h in range(DA_HEADS):
            sl = slice(h * DA_DV, (h + 1) * DA_DV)
            att = o[2 * h:2 * h + 1, sl] - lam * o[2 * h + 1:2 * h + 2, sl]
            o_ref[:, sl] = _subln_gate(att, subg, g[:, sl], lam_init)


class _DecodeJob(NamedTuple):
    page_table: jax.Array
    cache_k: jax.Array
    cache_v: jax.Array
    small: jax.Array
    lam_init: float
    layer: int


DEC_SMALL_ROWS = 8


def _pack_decode_rows(q, k_new, v_new, g_da, lam_vecs, subg):
    shared = jnp.concatenate([*lam_vecs, subg], axis=-1)
    shared = jnp.pad(shared, ((0, 0), (0, D_DA - shared.shape[1])))
    rows = [q, k_new, v_new, g_da, jnp.broadcast_to(shared, (DEC_BATCH, D_DA))]
    rows += [jnp.zeros((DEC_BATCH, D_DA), F32)] * (DEC_SMALL_ROWS - len(rows))
    return jnp.stack(rows, axis=1)


class _DecodeOperands(NamedTuple):
    inputs: list
    in_specs: list
    out_shape: jax.ShapeDtypeStruct
    out_spec: pl.BlockSpec
    scratch_shapes: list


def _decode_operands(job, npg, first, n_seq, seq_and_step):
    small = pl.BlockSpec((None, DEC_SMALL_ROWS, D_DA), lambda *a: (seq_and_step(a[:-1])[0], 0, 0))

    def page_spec(i):
        def index(*a):
            b, step = seq_and_step(a[:-1])
            return (job.layer * N_POOL + a[-1][b, step * npg + i], 0, 0)
        return pl.BlockSpec((None, PAGE_SIZE * 2 * DA_HEADS, LANES), index)

    pages = [page_spec(i) for i in range(npg)]
    return _DecodeOperands(
        inputs=[job.small, *([job.cache_k] * npg), *([job.cache_v] * npg)],
        in_specs=[small] + pages + pages,
        out_shape=jax.ShapeDtypeStruct((n_seq, 1, D_DA), F32),
        out_spec=pl.BlockSpec((None, 1, D_DA), lambda *a: (seq_and_step(a[:-1])[0] - first, 0, 0)),
        scratch_shapes=[pltpu.VMEM((SAMPLE_ROWS, D_DA), BF16),
                        pltpu.VMEM((SAMPLE_ROWS, 1), F32), pltpu.VMEM((SAMPLE_ROWS, 1), F32),
                        pltpu.VMEM((SAMPLE_ROWS, D_DA), F32)])


def _dec_ret_kernel(st_ref, qc_ref, kc_ref, v_ref, g_ref, nst_ref, o_ref):
    v_all = v_ref[...]
    g_all = g_ref[...]
    for h in range(RET_HEADS):
        dec = float(1.0 - 2.0 ** (-5.0 - h))
        sl = slice(h * RET_DV, (h + 1) * RET_DV)
        st = st_ref[h]
        qc = qc_ref[h]
        kc = kc_ref[h]
        v = v_all[:, sl]
        nst_ref[h] = st * dec + kc * v
        cross = jnp.sum(qc * st, axis=0, keepdims=True) * dec
        intra = jnp.sum(qc * kc, axis=0, keepdims=True) * v
        o_ref[:, sl] = _rms_rows(intra + cross) * g_all[:, sl]


def _decode_retention(state, rq_col, rk_col, rv, g_ret, layer):
    col = pl.BlockSpec((None, RET_HEADS, RET_DK, 1), lambda b: (b, 0, 0, 0))
    row = pl.BlockSpec((None, 1, D_RET), lambda b: (b, 0, 0))
    return pl.pallas_call(
        _dec_ret_kernel,
        out_shape=[jax.ShapeDtypeStruct((DEC_BATCH, RET_HEADS, RET_DK, RET_DV), F32),
                   jax.ShapeDtypeStruct((DEC_BATCH, 1, D_RET), F32)],
        grid=(DEC_BATCH,),
        in_specs=[pl.BlockSpec((None, None, RET_HEADS, RET_DK, RET_DV), lambda b: (layer, b, 0, 0, 0)),
                  col, col, row, row],
        out_specs=[pl.BlockSpec((None, RET_HEADS, RET_DK, RET_DV), lambda b: (b, 0, 0, 0)),
                   pl.BlockSpec((None, 1, D_RET), lambda b: (b, 0, 0))],
        compiler_params=_cparams(1),
        name="decode_retention",
    )(state, rq_col, rk_col, rv, g_ret)


def _rotary_tables(pos):
    half = RET_DK // 2
    theta = 1.0 / (ROPE_BASE ** jnp.linspace(0.0, 1.0, half, dtype=F32))
    ang = pos.astype(F32)[:, None] * theta[None, :]
    cos, sin = jnp.cos(ang), jnp.sin(ang)
    return jnp.concatenate([cos, cos], axis=-1), jnp.concatenate([-sin, sin], axis=-1)


def _pad_rows(x):
    return jnp.pad(x, ((0, SAMPLE_ROWS - x.shape[0]), (0, 0)))


def kernel(x_prompt, x_sample, cache_k, cache_v, state_ret, page_table, norm_g, w_in, w_out,
           qn_g, kn_g, lam_q1, lam_k1, lam_q2, lam_k2, subln_g):
    t_p = BATCH * SEQ
    w_in_bf = w_in[0:1].astype(BF16)
    w_out_bf = w_out.astype(BF16)
    cos_p, sin_p = _rotary_tables(jnp.arange(SEQ))
    cos_s, sin_s = _rotary_tables(jnp.full((SAMPLE_ROWS,), PAST_LEN))
    ck = cache_k.reshape(DEPTH * N_POOL, PAGE_SIZE * DA_HEADS * 2, DA_HEAD_DIM)
    cv = _values_to_cache_order(cache_v)

    xp = x_prompt.reshape(t_p, D_MODEL)
    xs = _pad_rows(x_sample.reshape(DEC_BATCH, D_MODEL))
    hp = _input_norm(xp, norm_g[0][None], PROMPT_TM)
    hs = _input_norm(xs, norm_g[0][None], SAMPLE_ROWS)

    kp_all = vp_all = None
    sp_l, ks_l, vs_l, ss_l = [], [], [], []
    for l in range(DEPTH):
        lam_init = 0.8 - 0.6 * math.exp(-0.3 * l)
        lam_vecs = [a[l][None] for a in (lam_q1, lam_k1, lam_q2, lam_k2)]
        subg = subln_g[l][None]
        g_next = norm_g[l + 1][None] if l + 1 < DEPTH else None

        zs = _sample_project(hs, w_in_bf, qn_g[l][None], kn_g[l][None], cos_s, sin_s)
        seg = lambda col, width: zs[:, col:col + width]
        qs, k_row, v_row, g_da_s = (seg(c, D_DA) for c in (COL_Q, COL_K, COL_V, COL_GDA))
        rq_s, rk_s = seg(COL_RQ, RET_HEADS * RET_DK), seg(COL_RK, RET_HEADS * RET_DK)
        rv_s, g_ret_s = seg(COL_RV, D_RET), seg(COL_GRET, D_RET)
        as_row = lambda z: z[:DEC_BATCH].reshape(DEC_BATCH, 1, z.shape[-1])
        as_col = lambda z: z[:DEC_BATCH].reshape(DEC_BATCH, RET_HEADS, RET_DK, 1)
        small = _pack_decode_rows(qs[:DEC_BATCH], k_row[:DEC_BATCH], v_row[:DEC_BATCH], g_da_s[:DEC_BATCH],
                                  lam_vecs, subg)
        job = _DecodeJob(page_table, ck, cv, small, lam_init, l)

        q, kp_all, kbf, vp_all, vbf, g_da, rq, rk, rv, g_ret, dec_rows = _project(
            hp, w_in_bf, l, qn_g[l][None], kn_g[l][None], cos_p, sin_p, PROMPT_TM,
            kv_stacks=(kp_all, vp_all), decode_job=job)
        a, w_in_bf = _prompt_attention(q, kbf, vbf, g_da, lam_vecs, subg, lam_init,
                                       w_next=(w_in, l + 1) if l + 1 < DEPTH else None)
        r, st = _prompt_retention(rq, rk, rv, g_ret)
        xp, hp, dec_row = _merge(a, r, w_out_bf, l, xp, g_next, MERGE_TM, decode=(job, len(dec_rows)))
        dec_rows.append(dec_row)
        sp_l.append(st)

        assert len(dec_rows) == DEC_BATCH
        a = jnp.concatenate(dec_rows, axis=0)
        rq, rk, rv, g_ret = rq_s, rk_s, rv_s, g_ret_s
        nst, r = _decode_retention(state_ret, as_col(rq), as_col(rk), as_row(rv), as_row(g_ret), l)
        a16 = _pad_rows(a.reshape(DEC_BATCH, D_DA)).astype(BF16)
        r16 = _pad_rows(r.reshape(DEC_BATCH, D_RET)).astype(BF16)
        xs, hs, _ = _merge(a16, r16, w_out_bf, l, xs, g_next, SAMPLE_ROWS)
        ks_l.append(k_row[:DEC_BATCH].reshape(DEC_BATCH, DEC_SEQ, DA_HEADS, 2, DA_HEAD_DIM))
        vs_l.append(v_row[:DEC_BATCH].reshape(DEC_BATCH, DEC_SEQ, DA_HEADS, DA_DV))
        ss_l.append(nst)

    return (xp.reshape(BATCH, SEQ, D_MODEL),
            xs[:DEC_BATCH].reshape(DEC_BATCH, DEC_SEQ, D_MODEL),
            _keys_from_cache_order(kp_all, (DEPTH, BATCH, SEQ)),
            _values_from_cache_order(vp_all, (DEPTH, BATCH, SEQ)), jnp.stack(sp_l),
            jnp.stack(ks_l), jnp.stack(vs_l), jnp.stack(ss_l))
```

```python
import functools
import math
from typing import NamedTuple

import numpy as np
import jax
import jax.numpy as jnp
from jax import lax
from jax.experimental import pallas as pl
from jax.experimental.pallas import tpu as pltpu

D_MODEL = 2048
BATCH = 4
SEQ = 2048
DEPTH = 4
DEC_BATCH = 8
DEC_SEQ = 1
PAST_LEN = 16384
PAGE_SIZE = 128
N_PAGES = PAST_LEN // PAGE_SIZE
N_POOL = (DEC_BATCH * N_PAGES * 5) // 4

D_DA = D_MODEL // 2
D_RET = D_MODEL - D_DA
DA_HEAD_DIM = 128
DA_HEADS = D_DA // (2 * DA_HEAD_DIM)
DA_DV = 2 * DA_HEAD_DIM
RET_HEADS = 4
RET_DV = D_RET // RET_HEADS
RET_DK = RET_DV // 2
ROPE_BASE = 10000.0
EPS = 1e-6
D_IN = 4 * D_DA + 2 * RET_HEADS * RET_DK + 2 * D_RET

COL_Q, COL_K, COL_V, COL_GDA = 0, D_DA, 2 * D_DA, 3 * D_DA
COL_RQ = 4 * D_DA
COL_RK = COL_RQ + RET_HEADS * RET_DK
COL_RV = COL_RK + RET_HEADS * RET_DK
COL_GRET = COL_RV + D_RET

LANES = 128
SAMPLE_ROWS = 16
VMEM_LIMIT = 56 * 1024 * 1024

PROMPT_TM = 512
MERGE_TM = 512
N_COL_CHUNKS = 4
DEC_PAGE_GROUPS = 2
ATTN_TQ = 512
ATTN_HEADS_PER_STEP = 4
Q_SCALE = DA_HEAD_DIM ** -0.5 * math.log2(math.e)
RET_CHUNK = 128
RET_TILE = 1024

LOG_DECAY = np.log(1.0 - 2.0 ** (-5.0 - np.arange(RET_HEADS, dtype=np.float32))).astype(np.float32)

F32 = jnp.float32
BF16 = jnp.bfloat16


def _cparams(n_axes):
    return pltpu.CompilerParams(dimension_semantics=("arbitrary",) * n_axes,
                                vmem_limit_bytes=VMEM_LIMIT)


def _silu(g):
    return g / (1.0 + jnp.exp(-g))


def _rms_rows(x):
    return x * lax.rsqrt(jnp.mean(x * x, axis=-1, keepdims=True) + EPS)


def _dot_nt(a, b):
    return lax.dot_general(a, b, (((1,), (1,)), ((), ())), preferred_element_type=F32)


def _dot_tn(a, b):
    return lax.dot_general(a, b, (((0,), (0,)), ((), ())), preferred_element_type=F32)


def _norm_kernel(x_ref, g_ref, h_ref):
    h_ref[...] = (_rms_rows(x_ref[...]) * g_ref[...]).astype(BF16)


def _input_norm(x, g, tm):
    t = x.shape[0]
    return pl.pallas_call(
        _norm_kernel,
        out_shape=jax.ShapeDtypeStruct((t, D_MODEL), BF16),
        grid=(t // tm,),
        in_specs=[pl.BlockSpec((tm, D_MODEL), lambda m: (m, 0)),
                  pl.BlockSpec((1, D_MODEL), lambda m: (0, 0))],
        out_specs=pl.BlockSpec((tm, D_MODEL), lambda m: (m, 0)),
        compiler_params=_cparams(1),
        name="input_norm",
    )(x, g)


def _chunk_dots(h_ref, w_ref, n_chunks, epilogue):
    width = w_ref.shape[1] // n_chunks

    def part(c):
        z = jnp.dot(h_ref[...], w_ref[:, c * width:(c + 1) * width], preferred_element_type=F32)
        epilogue(c, z)

    return [functools.partial(part, c) for c in range(n_chunks)]


def _proj_qk_parts(h_ref, w_ref, g_ref, *out_refs, scale):
    n_grp = D_DA // DA_HEAD_DIM
    grp_per_chunk = n_grp // N_COL_CHUNKS

    def epilogue(c, z):
        tm = z.shape[0]
        g = g_ref[...]
        for jj in range(grp_per_chunk):
            j = c * grp_per_chunk + jj
            y = _rms_rows(z[:, jj * DA_HEAD_DIM:(jj + 1) * DA_HEAD_DIM]) * g
            if scale is not None:
                y = y * scale
            for o in out_refs:
                if o.shape[1] == DA_HEAD_DIM:
                    o[pl.ds(j, tm, stride=n_grp), :] = y.astype(o.dtype)
                else:
                    o[:, j * DA_HEAD_DIM:(j + 1) * DA_HEAD_DIM] = y.astype(o.dtype)

    return _chunk_dots(h_ref, w_ref, N_COL_CHUNKS, epilogue)


def _proj_v_parts(h_ref, w_ref, v_ref, vact_ref):
    assert N_COL_CHUNKS == DA_HEADS

    def epilogue(h, z):
        tm = z.shape[0]
        for j in range(DA_DV // LANES):
            v_ref[pl.ds(j * DA_HEADS + h, tm, stride=2 * DA_HEADS), :] = z[:, j * LANES:(j + 1) * LANES]
        vact_ref[:, h * DA_DV:(h + 1) * DA_DV] = z.astype(vact_ref.dtype)

    return _chunk_dots(h_ref, w_ref, N_COL_CHUNKS, epilogue)


def _proj_copy_parts(h_ref, w_ref, *out_refs):
    def epilogue(c, z):
        width = z.shape[1]
        for o in out_refs:
            o[:, c * width:(c + 1) * width] = z.astype(o.dtype)

    return _chunk_dots(h_ref, w_ref, N_COL_CHUNKS, epilogue)


def _proj_gate_parts(h_ref, w_ref, o_ref):
    def epilogue(c, z):
        width = z.shape[1]
        o_ref[:, c * width:(c + 1) * width] = _silu(z).astype(o_ref.dtype)

    return _chunk_dots(h_ref, w_ref, N_COL_CHUNKS, epilogue)


def _proj_rot_parts(h_ref, w_ref, cos_ref, sin_ref, rq_ref, rk_ref):
    grp_per_chunk = 2 * RET_HEADS // N_COL_CHUNKS

    def epilogue(c, z):
        cos2 = cos_ref[...]
        sin2 = sin_ref[...]
        for jj in range(grp_per_chunk):
            j = c * grp_per_chunk + jj
            x = z[:, jj * RET_DK:(jj + 1) * RET_DK]
            if j >= RET_HEADS:
                x = x * (RET_DK ** -0.5)
            y = x * cos2 + pltpu.roll(x, RET_DK // 2, axis=1) * sin2
            if j < RET_HEADS:
                rq_ref[:, j * RET_DK:(j + 1) * RET_DK] = y.astype(rq_ref.dtype)
            else:
                jk = j - RET_HEADS
                rk_ref[:, jk * RET_DK:(jk + 1) * RET_DK] = y.astype(rk_ref.dtype)

    return _chunk_dots(h_ref, w_ref, N_COL_CHUNKS, epilogue)


def _run_parts(parts_fn):
    def kern(*refs):
        for part in parts_fn(*refs):
            part()
    return kern


def _with_carried_buffer(parts_fn):
    def wrapped(carried_ref, *refs):
        del carried_ref
        return parts_fn(*refs)
    return wrapped


def _with_decode(parts_fn, dec_kern, n_tile_in, n_dec_in, n_tile_out):
    def wrapped(pt_ref, *refs):
        tile_in = refs[:n_tile_in]
        dec_in = refs[n_tile_in:n_tile_in + n_dec_in]
        outs = refs[n_tile_in + n_dec_in:]
        dec_kern(pt_ref, *dec_in, *outs[n_tile_out:], co_parts=parts_fn(*tile_in, *outs[:n_tile_out]))
    return wrapped


def _proj_call(kern, h, w_in, layer, col0, ncols, tm, extra, extra_specs, outs, name, stacked=None,
               decode=None):
    t = h.shape[0]
    nm = t // tm
    assert col0 % ncols == 0
    cb = col0 // ncols
    out_shape = [jax.ShapeDtypeStruct((t * r, w), dt) for (r, w, dt) in outs]
    out_specs = [pl.BlockSpec((tm * r, w), lambda m, *_: (m, 0)) for (r, w, _) in outs]
    inputs = [h, w_in, *extra]
    in_specs = [pl.BlockSpec((tm, D_MODEL), lambda m, *_: (m, 0)),
                pl.BlockSpec((None, D_MODEL, ncols), lambda m, *_: (0, 0, cb))] + extra_specs
    aliases = {}
    if stacked is not None:
        i, buf = stacked
        r, w, dt = outs[i]
        out_shape[i] = jax.ShapeDtypeStruct((DEPTH * t * r, w), dt)
        out_specs[i] = pl.BlockSpec((tm * r, w), lambda m, *_: (layer * nm + m, 0))
        if buf is not None:
            kern = _with_carried_buffer(kern)
            inputs = [buf] + inputs
            in_specs = [pl.BlockSpec(memory_space=pl.ANY)] + in_specs
            aliases = {0: i}
    return _launch_rows(kern, nm, inputs, in_specs, out_shape, out_specs, aliases, name, decode)


def _launch_rows(kern, nm, inputs, in_specs, out_shape, out_specs, aliases, name, decode):
    if decode is None:
        return pl.pallas_call(
            _run_parts(kern),
            out_shape=out_shape,
            grid=(nm,),
            in_specs=in_specs,
            out_specs=out_specs,
            input_output_aliases=aliases,
            compiler_params=_cparams(1),
            name=name,
        )(*inputs)

    job, b = decode
    npg = N_PAGES // nm
    assert npg * nm == N_PAGES
    dec = _decode_operands(job, npg, b, 1, lambda idx: (b, idx[0]))
    body = _with_decode(kern, functools.partial(_dec_attn_kernel, lam_init=job.lam_init, npg=npg, step_axis=0),
                        len(inputs), len(dec.inputs), len(out_shape))
    grid_spec = pltpu.PrefetchScalarGridSpec(
        num_scalar_prefetch=1,
        grid=(nm,),
        in_specs=in_specs + dec.in_specs,
        out_specs=out_specs + [dec.out_spec],
        scratch_shapes=dec.scratch_shapes,
    )
    return pl.pallas_call(
        body,
        out_shape=out_shape + [dec.out_shape],
        grid_spec=grid_spec,
        input_output_aliases={k + 1: v for k, v in aliases.items()},
        compiler_params=_cparams(1),
        name=name + "_dec",
    )(job.page_table, *inputs, *dec.inputs)


N_PROJ_CALLS = 7


def _project(h, w_in, layer, qn_g, kn_g, cos2, sin2, tm, kv_stacks, decode_job):
    act_dtype = BF16
    k_stacked = (0, kv_stacks[0])
    v_stacked = (0, kv_stacks[1])
    gspec = [pl.BlockSpec((1, DA_HEAD_DIM), lambda m, *_: (0, 0))]
    cache_rows = 2 * DA_HEADS
    dec_rows = []

    def call(kern, col0, ncols, extra, extra_specs, outs, name, stacked=None):
        res = _proj_call(kern, h, w_in, layer, col0, ncols, tm, extra, extra_specs, outs, name,
                         stacked=stacked, decode=(decode_job, len(dec_rows)))
        dec_rows.append(res[-1])
        return res[:-1]

    (q,) = call(functools.partial(_proj_qk_parts, scale=Q_SCALE), COL_Q, D_DA, [qn_g], gspec,
                [(1, D_DA, act_dtype)], "proj_q")
    k32, kact = call(functools.partial(_proj_qk_parts, scale=None), COL_K, D_DA, [kn_g], gspec,
                     [(cache_rows, DA_HEAD_DIM, F32), (1, D_DA, act_dtype)], "proj_k", stacked=k_stacked)
    v32, vact = call(_proj_v_parts, COL_V, D_DA, [], [],
                     [(cache_rows, LANES, F32), (1, D_DA, act_dtype)], "proj_v", stacked=v_stacked)
    (g_da,) = call(_proj_gate_parts, COL_GDA, D_DA, [], [], [(1, D_DA, act_dtype)], "proj_gda")
    n_tab = cos2.shape[0] // tm
    tspec = [pl.BlockSpec((tm, RET_DK), lambda m, *_: (m % n_tab, 0))] * 2
    nrot = 2 * RET_HEADS * RET_DK
    rq, rk = call(_proj_rot_parts, COL_RQ, nrot, [cos2, sin2], tspec,
                  [(1, nrot // 2, act_dtype), (1, nrot // 2, act_dtype)], "proj_rot")
    (rv,) = call(_proj_copy_parts, COL_RV, D_RET, [], [], [(1, D_RET, act_dtype)], "proj_rv")
    (g_ret,) = call(_proj_gate_parts, COL_GRET, D_RET, [], [], [(1, D_RET, act_dtype)], "proj_gret")
    assert len(dec_rows) == N_PROJ_CALLS
    return q, k32, kact, v32, vact, g_da, rq, rk, rv, g_ret, dec_rows


SAMPLE_SEG = D_DA


def _sample_proj_kernel(h_ref, w_ref, qg_ref, kg_ref, cos_ref, sin_ref, o_ref):
    j = pl.program_id(0)
    z = jnp.dot(h_ref[...], w_ref[...], preferred_element_type=F32)

    def norm_groups(g, scale):
        for grp in range(SAMPLE_SEG // DA_HEAD_DIM):
            sl = slice(grp * DA_HEAD_DIM, (grp + 1) * DA_HEAD_DIM)
            y = _rms_rows(z[:, sl]) * g
            o_ref[:, sl] = y * scale if scale is not None else y

    @pl.when(j == COL_Q // SAMPLE_SEG)
    def _():
        norm_groups(qg_ref[...], Q_SCALE)

    @pl.when(j == COL_K // SAMPLE_SEG)
    def _():
        norm_groups(kg_ref[...], None)

    @pl.when((j == COL_V // SAMPLE_SEG) | (j == COL_RV // SAMPLE_SEG))
    def _():
        o_ref[...] = z

    @pl.when((j == COL_GDA // SAMPLE_SEG) | (j == COL_GRET // SAMPLE_SEG))
    def _():
        o_ref[...] = _silu(z)

    @pl.when(j == COL_RQ // SAMPLE_SEG)
    def _():
        cos2 = cos_ref[...]
        sin2 = sin_ref[...]
        for grp in range(2 * RET_HEADS):
            sl = slice(grp * RET_DK, (grp + 1) * RET_DK)
            x = z[:, sl]
            if grp >= RET_HEADS:
                x = x * (RET_DK ** -0.5)
            o_ref[:, sl] = x * cos2 + pltpu.roll(x, RET_DK // 2, axis=1) * sin2


def _sample_project(h, w_in, qn_g, kn_g, cos2, sin2):
    assert 2 * RET_HEADS * RET_DK == SAMPLE_SEG and COL_RK == COL_RQ + RET_HEADS * RET_DK
    rows = h.shape[0]
    const = lambda shape: pl.BlockSpec(shape, lambda j: (0,) * len(shape))
    return pl.pallas_call(
        _sample_proj_kernel,
        out_shape=jax.ShapeDtypeStruct((rows, D_IN), F32),
        grid=(D_IN // SAMPLE_SEG,),
        in_specs=[const((rows, D_MODEL)),
                  pl.BlockSpec((None, D_MODEL, SAMPLE_SEG), lambda j: (0, 0, j)),
                  const((1, DA_HEAD_DIM)), const((1, DA_HEAD_DIM)),
                  const((rows, RET_DK)), const((rows, RET_DK))],
        out_specs=pl.BlockSpec((rows, SAMPLE_SEG), lambda j: (0, j)),
        compiler_params=_cparams(1),
        name="sample_proj",
    )(h, w_in, qn_g, kn_g, cos2, sin2)


def _keys_from_cache_order(k32, lead):
    return k32.reshape(lead + (DA_HEADS, 2, DA_HEAD_DIM))


def _values_from_cache_order(v32, lead):
    n_half = DA_DV // LANES
    v = v32.reshape(lead + (n_half, DA_HEADS, LANES))
    nd = len(lead)
    v = v.transpose(tuple(range(nd)) + (nd + 1, nd, nd + 2))
    return v.reshape(lead + (DA_HEADS, DA_DV))


def _values_to_cache_order(cache_v):
    n_half = DA_DV // LANES
    v = cache_v.reshape(-1, PAGE_SIZE, DA_HEADS, n_half, LANES).transpose(0, 1, 3, 2, 4)
    return v.reshape(-1, PAGE_SIZE * n_half * DA_HEADS, LANES)


def _lam_value(lq1, lk1, lq2, lk2, lam_init):
    s1 = jnp.sum(lq1 * lk1, axis=-1, keepdims=True)
    s2 = jnp.sum(lq2 * lk2, axis=-1, keepdims=True)
    return jnp.exp(s1) - jnp.exp(s2) + lam_init


def _subln_gate(o, subg, gate, lam_init):
    return (_rms_rows(o) * subg) * (1.0 - lam_init) * gate.astype(F32)


def _attn_with_weight_cast(lq1_ref, lk1_ref, lq2_ref, lk2_ref, subg_ref, q_ref, k_ref, v_ref, g_ref,
                           w32_ref, o_ref, w16_ref, m_ref, l_ref, acc_ref, **kw):
    w16_ref[...] = w32_ref[...].astype(w16_ref.dtype)
    _attn_kernel(lq1_ref, lk1_ref, lq2_ref, lk2_ref, subg_ref, q_ref, k_ref, v_ref, g_ref, o_ref,
                 m_ref, l_ref, acc_ref, **kw)


def _attn_kernel(lq1_ref, lk1_ref, lq2_ref, lk2_ref, subg_ref, q_ref, k_ref, v_ref, g_ref, o_ref,
                 m_ref, l_ref, acc_ref, *, lam_init, tq, hps):
    qi = pl.program_id(2)
    q = q_ref[...]
    d = DA_HEAD_DIM

    m_ref[...] = jnp.full(m_ref.shape, -jnp.inf, F32)
    l_ref[...] = jnp.zeros(l_ref.shape, F32)
    acc_ref[...] = jnp.zeros(acc_ref.shape, F32)

    def step(r0, nr, k, v, mask):
        rows = slice(r0, r0 + nr)
        n_lt = k.shape[0] // LANES
        upd = []
        for ci in range(2 * hps):
            hh = ci // 2
            s = _dot_nt(q[rows, ci * d:(ci + 1) * d], k[:, ci * d:(ci + 1) * d])
            if mask is not None:
                s = jnp.where(mask, s, -jnp.inf)
            st = [s[:, j * LANES:(j + 1) * LANES] for j in range(n_lt)]
            fold = st[0]
            for j in range(1, n_lt):
                fold = jnp.maximum(fold, st[j])
            m_prev = m_ref[ci, rows]
            m_new = jnp.maximum(m_prev, jnp.broadcast_to(jnp.max(fold, axis=-1, keepdims=True),
                                                         (nr, LANES)))
            alpha = jnp.exp2(m_prev - m_new)
            pt = [jnp.exp2(t - m_new) for t in st]
            psum = pt[0]
            for j in range(1, n_lt):
                psum = psum + pt[j]
            p = jnp.concatenate([t.astype(BF16) for t in pt], axis=-1)
            pv = jnp.dot(p, v[:, hh * DA_DV:(hh + 1) * DA_DV], preferred_element_type=F32)
            upd.append((m_new, alpha, psum, pv))
        for ci, (m_new, alpha, psum, pv) in enumerate(upd):
            m_ref[ci, rows] = m_new
            l_ref[ci, rows] = alpha * l_ref[ci, rows] + psum
            acc_ref[ci, rows] = jnp.concatenate([alpha] * (DA_DV // LANES), axis=-1) * acc_ref[ci, rows] + pv

    def body(ki, carry):
        off = pl.multiple_of(ki * tq, tq)
        step(0, tq, k_ref[pl.ds(off, tq), :], v_ref[pl.ds(off, tq), :], None)
        return carry

    lax.fori_loop(0, qi, body, 0)

    off = pl.multiple_of(qi * tq, tq)
    hq = tq // 2
    for r0, nk in ((0, hq), (hq, tq)):
        row = lax.broadcasted_iota(jnp.int32, (hq, nk), 0) + r0
        col = lax.broadcasted_iota(jnp.int32, (hq, nk), 1)
        step(r0, hq, k_ref[pl.ds(off, nk), :], v_ref[pl.ds(off, nk), :], col <= row)

    lam = _lam_value(lq1_ref[...], lk1_ref[...], lq2_ref[...], lk2_ref[...], lam_init)
    for hh in range(hps):
        cols = slice(hh * DA_DV, (hh + 1) * DA_DV)
        l1 = jnp.sum(l_ref[2 * hh], axis=-1, keepdims=True)
        l2 = jnp.sum(l_ref[2 * hh + 1], axis=-1, keepdims=True)
        o = acc_ref[2 * hh] / l1 - lam * (acc_ref[2 * hh + 1] / l2)
        o_ref[:, cols] = _subln_gate(o, subg_ref[...], g_ref[:, cols], lam_init).astype(o_ref.dtype)


def _prompt_attention(q, k, v, g_da, lam_vecs, subg, lam_init, w_next=None):
    tq = ATTN_TQ
    nq = SEQ // tq
    hps = ATTN_HEADS_PER_STEP
    nh = DA_HEADS // hps
    width = hps * DA_DV
    vec = pl.BlockSpec((1, DA_HEAD_DIM), lambda b, h, i: (0, 0))
    kern = _attn_kernel
    inputs = [*lam_vecs, subg, q, k, v, g_da]
    in_specs = [vec, vec, vec, vec,
                pl.BlockSpec((1, DA_DV), lambda b, h, i: (0, 0)),
                pl.BlockSpec((tq, width), lambda b, h, i: (b * nq + i, h)),
                pl.BlockSpec((SEQ, width), lambda b, h, i: (b, h)),
                pl.BlockSpec((SEQ, width), lambda b, h, i: (b, h)),
                pl.BlockSpec((tq, width), lambda b, h, i: (b * nq + i, h))]
    out_shape = [jax.ShapeDtypeStruct((BATCH * SEQ, D_DA), BF16)]
    out_specs = [pl.BlockSpec((tq, width), lambda b, h, i: (b * nq + i, h))]
    if w_next is not None:
        w_in, layer = w_next
        n_steps = BATCH * nh * nq
        rows = D_MODEL // n_steps
        assert rows * n_steps == D_MODEL
        kern = _attn_with_weight_cast
        inputs.append(w_in)
        in_specs.append(pl.BlockSpec((None, rows, D_IN), lambda b, h, i: (layer, (b * nh + h) * nq + i, 0)))
        out_shape.append(jax.ShapeDtypeStruct((1, D_MODEL, D_IN), BF16))
        out_specs.append(pl.BlockSpec((None, rows, D_IN), lambda b, h, i: (0, (b * nh + h) * nq + i, 0)))
    res = pl.pallas_call(
        functools.partial(kern, lam_init=lam_init, tq=tq, hps=hps),
        out_shape=out_shape,
        grid=(BATCH, nh, nq),
        in_specs=in_specs,
        out_specs=out_specs,
        scratch_shapes=[pltpu.VMEM((2 * hps, tq, LANES), F32), pltpu.VMEM((2 * hps, tq, LANES), F32),
                        pltpu.VMEM((2 * hps, tq, DA_DV), F32)],
        compiler_params=_cparams(3),
        name="prompt_attention",
    )(*inputs)
    return (res[0], res[1]) if w_next is not None else (res[0], None)


def _ret_kernel(q_ref, k_ref, v_ref, g_ref, o_ref, st_ref, *, chunk):
    c = chunk
    n_chunks = q_ref.shape[0] // c

    @pl.when(pl.program_id(1) == 0)
    def _():
        st_ref[...] = jnp.zeros(st_ref.shape, F32)

    li = lax.broadcasted_iota(jnp.int32, (c, 1), 0).astype(F32)
    diff = (lax.broadcasted_iota(jnp.int32, (c, c), 0)
            - lax.broadcasted_iota(jnp.int32, (c, c), 1)).astype(F32)
    decays = []
    for h in range(RET_HEADS):
        lg = float(LOG_DECAY[h])
        decays.append((jnp.where(diff >= 0, jnp.exp(lg * jnp.maximum(diff, 0.0)), 0.0),
                       jnp.exp((li + 1.0) * lg), jnp.exp((c - 1.0 - li) * lg), math.exp(c * lg)))

    def body(i, carry):
        off = pl.multiple_of(i * c, c)
        for h in range(RET_HEADS):
            dmask, cross_dec, k_dec, st_dec = decays[h]
            q = q_ref[pl.ds(off, c), h * RET_DK:(h + 1) * RET_DK]
            k = k_ref[pl.ds(off, c), h * RET_DK:(h + 1) * RET_DK]
            v = v_ref[pl.ds(off, c), h * RET_DV:(h + 1) * RET_DV]
            state = st_ref[h]
            s = _dot_nt(q, k) * dmask
            intra = jnp.dot(s.astype(BF16), v, preferred_element_type=F32)
            cross = jnp.dot(q, state.astype(BF16), preferred_element_type=F32) * cross_dec
            kd = (k.astype(F32) * k_dec).astype(BF16)
            st_ref[h] = state * st_dec + _dot_tn(kd, v)
            gate = g_ref[pl.ds(off, c), h * RET_DV:(h + 1) * RET_DV].astype(F32)
            o_ref[pl.ds(off, c), h * RET_DV:(h + 1) * RET_DV] = (
                _rms_rows(intra + cross) * gate).astype(o_ref.dtype)
        return carry

    lax.fori_loop(0, n_chunks, body, 0)


def _prompt_retention(rq, rk, rv, g_ret):
    ts = RET_TILE
    ns = SEQ // ts
    nqk = RET_HEADS * RET_DK
    return pl.pallas_call(
        functools.partial(_ret_kernel, chunk=RET_CHUNK),
        out_shape=[jax.ShapeDtypeStruct((BATCH * SEQ, D_RET), BF16),
                   jax.ShapeDtypeStruct((BATCH, RET_HEADS, RET_DK, RET_DV), F32)],
        grid=(BATCH, ns),
        in_specs=[pl.BlockSpec((ts, nqk), lambda b, i: (b * ns + i, 0)),
                  pl.BlockSpec((ts, nqk), lambda b, i: (b * ns + i, 0)),
                  pl.BlockSpec((ts, D_RET), lambda b, i: (b * ns + i, 0)),
                  pl.BlockSpec((ts, D_RET), lambda b, i: (b * ns + i, 0))],
        out_specs=[pl.BlockSpec((ts, D_RET), lambda b, i: (b * ns + i, 0)),
                   pl.BlockSpec((None, RET_HEADS, RET_DK, RET_DV), lambda b, i: (b, 0, 0, 0))],
        compiler_params=_cparams(2),
        name="prompt_retention",
    )(rq, rk, rv, g_ret)


def _merge_parts(a_ref, r_ref, w_ref, x_ref, g_ref, y_ref, *h_ref):
    width = D_MODEL // N_COL_CHUNKS

    def part(c):
        cols = slice(c * width, (c + 1) * width)
        ar = jnp.concatenate([a_ref[...], r_ref[...]], axis=-1)
        y_ref[:, cols] = x_ref[:, cols] + jnp.dot(ar, w_ref[:, cols], preferred_element_type=F32)

    def norm_part():
        h_ref[0][...] = (_rms_rows(y_ref[...]) * g_ref[...]).astype(BF16)

    parts = [functools.partial(part, c) for c in range(N_COL_CHUNKS)]
    return parts + [norm_part] if h_ref else parts


def _merge(a, r, w_out, layer, x, g_next, tm, decode=None):
    t = x.shape[0]
    with_h = g_next is not None
    out_shape = [jax.ShapeDtypeStruct((t, D_MODEL), F32)]
    out_specs = [pl.BlockSpec((tm, D_MODEL), lambda m, *_: (m, 0))]
    if with_h:
        out_shape.append(jax.ShapeDtypeStruct((t, D_MODEL), BF16))
        out_specs.append(pl.BlockSpec((tm, D_MODEL), lambda m, *_: (m, 0)))
    else:
        g_next = jnp.ones((1, D_MODEL), F32)
    in_specs = [pl.BlockSpec((tm, D_DA), lambda m, *_: (m, 0)),
                pl.BlockSpec((tm, D_RET), lambda m, *_: (m, 0)),
                pl.BlockSpec((None, D_MODEL, D_MODEL), lambda m, *_: (layer, 0, 0),
                             pipeline_mode=pl.Buffered(1)),
                pl.BlockSpec((tm, D_MODEL), lambda m, *_: (m, 0)),
                pl.BlockSpec((1, D_MODEL), lambda m, *_: (0, 0))]
    res = list(_launch_rows(_merge_parts, t // tm, [a, r, w_out, x, g_next], in_specs, out_shape, out_specs,
                            {}, "merge", decode))
    dec_row = res.pop() if decode is not None else None
    return res[0], (res[1] if with_h else None), dec_row


def _dec_attn_kernel(pt_ref, small_ref, *rest, lam_init, npg, step_axis, co_parts):
    del pt_ref
    q_ref, kn_ref, vn_ref, g_ref = (small_ref.at[i:i + 1, :] for i in range(4))
    lq1_ref, lk1_ref, lq2_ref, lk2_ref = (
        small_ref.at[4:5, i * DA_HEAD_DIM:(i + 1) * DA_HEAD_DIM] for i in range(4))
    subg_ref = small_ref.at[4:5, 4 * DA_HEAD_DIM:4 * DA_HEAD_DIM + DA_DV]
    k_refs = rest[:npg]
    v_refs = rest[npg:2 * npg]
    o_ref = rest[2 * npg]
    qm_ref, m_ref, l_ref, acc_ref = rest[2 * npg + 1:]
    p_idx = pl.program_id(step_axis)
    rows = SAMPLE_ROWS

    @pl.when(p_idx == 0)
    def _():
        r = lax.broadcasted_iota(jnp.int32, (rows, D_DA), 0)
        cgrp = lax.broadcasted_iota(jnp.int32, (rows, D_DA), 1) // DA_HEAD_DIM
        qb = jnp.broadcast_to(q_ref[...], (rows, D_DA))
        qm_ref[...] = jnp.where(r == cgrp, qb, 0.0).astype(BF16)
        m_ref[...] = jnp.full(m_ref.shape, -jnp.inf, F32)
        l_ref[...] = jnp.zeros(l_ref.shape, F32)
        acc_ref[...] = jnp.zeros(acc_ref.shape, F32)

    def score_part(ks, ntok_valid):
        qm = qm_ref[...]
        s = jnp.concatenate([_dot_nt(qm, k) for k in ks], axis=-1)
        if ntok_valid is not None:
            tcol = lax.broadcasted_iota(jnp.int32, s.shape, 1)
            s = jnp.where(tcol < ntok_valid, s, -jnp.inf)
        m_prev = m_ref[...]
        m_new = jnp.maximum(m_prev, jnp.max(s, axis=-1, keepdims=True))
        alpha = jnp.exp2(m_prev - m_new)
        p = jnp.exp2(s - m_new)
        l_ref[...] = alpha * l_ref[...] + jnp.sum(p, axis=-1, keepdims=True)
        m_ref[...] = m_new
        return alpha, p.astype(BF16)

    def value_part(alpha, pb, vs):
        ntok = pb.shape[1] // len(vs)
        pv = jnp.dot(pb[:, :ntok], vs[0], preferred_element_type=F32)
        for i in range(1, len(vs)):
            pv = pv + jnp.dot(pb[:, i * ntok:(i + 1) * ntok], vs[i], preferred_element_type=F32)
        acc_ref[...] = alpha * acc_ref[...] + pv

    def load_k(r):
        n_hc = 2 * DA_HEADS
        return jnp.concatenate(
            [r[pl.ds(hc, PAGE_SIZE, stride=n_hc), :].astype(BF16) for hc in range(n_hc)], axis=-1)

    def load_v(r):
        n_half = DA_DV // LANES
        return jnp.concatenate(
            [r[pl.ds(j * DA_HEADS + h, PAGE_SIZE, stride=n_half * DA_HEADS), :].astype(BF16)
             for h in range(DA_HEADS) for j in range(n_half)], axis=-1)

    assert npg % DEC_PAGE_GROUPS == 0
    per = npg // DEC_PAGE_GROUPS
    carried = {}
    dec_parts = []
    for gi in range(DEC_PAGE_GROUPS):
        grp = slice(gi * per, (gi + 1) * per)

        def score(grp=grp):
            carried["alpha"], carried["p"] = score_part([load_k(r) for r in k_refs[grp]], None)

        def value(grp=grp):
            value_part(carried["alpha"], carried["p"], [load_v(r) for r in v_refs[grp]])

        dec_parts += [score, value]
    co_parts = list(co_parts)
    for i in range(max(len(co_parts), len(dec_parts))):
        if i < len(co_parts):
            co_parts[i]()
        if i < len(dec_parts):
            dec_parts[i]()

    @pl.when(p_idx == pl.num_programs(step_axis) - 1)
    def _():
        r = lax.broadcasted_iota(jnp.int32, (rows, D_DA), 0)
        kn = jnp.where(r == 0, jnp.broadcast_to(kn_ref[...], (rows, D_DA)), 0.0).astype(BF16)
        vn = jnp.where(r == 0, jnp.broadcast_to(vn_ref[...], (rows, D_DA)), 0.0).astype(BF16)
        value_part(*score_part([kn], 1), [vn])
        lam = _lam_value(lq1_ref[...], lk1_ref[...], lq2_ref[...], lk2_ref[...], lam_init)
        o = acc_ref[...] / l_ref[...]
        g = g_ref[...]
        subg = subg_ref[...]
        for h in range(DA_HEADS):
            sl = slice(h * DA_DV, (h + 1) * DA_DV)
            att = o[2 * h:2 * h + 1, sl] - lam * o[2 * h + 1:2 * h + 2, sl]
            o_ref[:, sl] = _subln_gate(att, subg, g[:, sl], lam_init)


class _DecodeJob(NamedTuple):
    page_table: jax.Array
    cache_k: jax.Array
    cache_v: jax.Array
    small: jax.Array
    lam_init: float
    layer: int


DEC_SMALL_ROWS = 8


def _pack_decode_rows(q, k_new, v_new, g_da, lam_vecs, subg):
    shared = jnp.concatenate([*lam_vecs, subg], axis=-1)
    shared = jnp.pad(shared, ((0, 0), (0, D_DA - shared.shape[1])))
    rows = [q, k_new, v_new, g_da, jnp.broadcast_to(shared, (DEC_BATCH, D_DA))]
    rows += [jnp.zeros((DEC_BATCH, D_DA), F32)] * (DEC_SMALL_ROWS - len(rows))
    return jnp.stack(rows, axis=1)


class _DecodeOperands(NamedTuple):
    inputs: list
    in_specs: list
    out_shape: jax.ShapeDtypeStruct
    out_spec: pl.BlockSpec
    scratch_shapes: list


def _decode_operands(job, npg, first, n_seq, seq_and_step):
    small = pl.BlockSpec((None, DEC_SMALL_ROWS, D_DA), lambda *a: (seq_and_step(a[:-1])[0], 0, 0))

    def page_spec(i):
        def index(*a):
            b, step = seq_and_step(a[:-1])
            return (job.layer * N_POOL + a[-1][b, step * npg + i], 0, 0)
        return pl.BlockSpec((None, PAGE_SIZE * 2 * DA_HEADS, LANES), index)

    pages = [page_spec(i) for i in range(npg)]
    return _DecodeOperands(
        inputs=[job.small, *([job.cache_k] * npg), *([job.cache_v] * npg)],
        in_specs=[small] + pages + pages,
        out_shape=jax.ShapeDtypeStruct((n_seq, 1, D_DA), F32),
        out_spec=pl.BlockSpec((None, 1, D_DA), lambda *a: (seq_and_step(a[:-1])[0] - first, 0, 0)),
        scratch_shapes=[pltpu.VMEM((SAMPLE_ROWS, D_DA), BF16),
                        pltpu.VMEM((SAMPLE_ROWS, 1), F32), pltpu.VMEM((SAMPLE_ROWS, 1), F32),
                        pltpu.VMEM((SAMPLE_ROWS, D_DA), F32)])


def _dec_ret_kernel(st_ref, qc_ref, kc_ref, v_ref, g_ref, nst_ref, o_ref):
    v_all = v_ref[...]
    g_all = g_ref[...]
    for h in range(RET_HEADS):
        dec = float(1.0 - 2.0 ** (-5.0 - h))
        sl = slice(h * RET_DV, (h + 1) * RET_DV)
        st = st_ref[h]
        qc = qc_ref[h]
        kc = kc_ref[h]
        v = v_all[:, sl]
        nst_ref[h] = st * dec + kc * v
        cross = jnp.sum(qc * st, axis=0, keepdims=True) * dec
        intra = jnp.sum(qc * kc, axis=0, keepdims=True) * v
        o_ref[:, sl] = _rms_rows(intra + cross) * g_all[:, sl]


def _decode_retention(state, rq_col, rk_col, rv, g_ret, layer):
    col = pl.BlockSpec((None, RET_HEADS, RET_DK, 1), lambda b: (b, 0, 0, 0))
    row = pl.BlockSpec((None, 1, D_RET), lambda b: (b, 0, 0))
    return pl.pallas_call(
        _dec_ret_kernel,
        out_shape=[jax.ShapeDtypeStruct((DEC_BATCH, RET_HEADS, RET_DK, RET_DV), F32),
                   jax.ShapeDtypeStruct((DEC_BATCH, 1, D_RET), F32)],
        grid=(DEC_BATCH,),
        in_specs=[pl.BlockSpec((None, None, RET_HEADS, RET_DK, RET_DV), lambda b: (layer, b, 0, 0, 0)),
                  col, col, row, row],
        out_specs=[pl.BlockSpec((None, RET_HEADS, RET_DK, RET_DV), lambda b: (b, 0, 0, 0)),
                   pl.BlockSpec((None, 1, D_RET), lambda b: (b, 0, 0))],
        compiler_params=_cparams(1),
        name="decode_retention",
    )(state, rq_col, rk_col, rv, g_ret)


def _rotary_tables(pos):
    half = RET_DK // 2
    theta = 1.0 / (ROPE_BASE ** jnp.linspace(0.0, 1.0, half, dtype=F32))
    ang = pos.astype(F32)[:, None] * theta[None, :]
    cos, sin = jnp.cos(ang), jnp.sin(ang)
    return jnp.concatenate([cos, cos], axis=-1), jnp.concatenate([-sin, sin], axis=-1)


def _pad_rows(x):
    return jnp.pad(x, ((0, SAMPLE_ROWS - x.shape[0]), (0, 0)))


def kernel(x_prompt, x_sample, cache_k, cache_v, state_ret, page_table, norm_g, w_in, w_out,
           qn_g, kn_g, lam_q1, lam_k1, lam_q2, lam_k2, subln_g):
    t_p = BATCH * SEQ
    w_in_bf = w_in[0:1].astype(BF16)
    w_out_bf = w_out.astype(BF16)
    cos_p, sin_p = _rotary_tables(jnp.arange(SEQ))
    cos_s, sin_s = _rotary_tables(jnp.full((SAMPLE_ROWS,), PAST_LEN))
    ck = cache_k.reshape(DEPTH * N_POOL, PAGE_SIZE * DA_HEADS * 2, DA_HEAD_DIM)
    cv = _values_to_cache_order(cache_v)

    xp = x_prompt.reshape(t_p, D_MODEL)
    xs = _pad_rows(x_sample.reshape(DEC_BATCH, D_MODEL))
    hp = _input_norm(xp, norm_g[0][None], PROMPT_TM)
    hs = _input_norm(xs, norm_g[0][None], SAMPLE_ROWS)

    kp_all = vp_all = None
    sp_l, ks_l, vs_l, ss_l = [], [], [], []
    for l in range(DEPTH):
        lam_init = 0.8 - 0.6 * math.exp(-0.3 * l)
        lam_vecs = [a[l][None] for a in (lam_q1, lam_k1, lam_q2, lam_k2)]
        subg = subln_g[l][None]
        g_next = norm_g[l + 1][None] if l + 1 < DEPTH else None

        zs = _sample_project(hs, w_in_bf, qn_g[l][None], kn_g[l][None], cos_s, sin_s)
        seg = lambda col, width: zs[:, col:col + width]
        qs, k_row, v_row, g_da_s = (seg(c, D_DA) for c in (COL_Q, COL_K, COL_V, COL_GDA))
        rq_s, rk_s = seg(COL_RQ, RET_HEADS * RET_DK), seg(COL_RK, RET_HEADS * RET_DK)
        rv_s, g_ret_s = seg(COL_RV, D_RET), seg(COL_GRET, D_RET)
        as_row = lambda z: z[:DEC_BATCH].reshape(DEC_BATCH, 1, z.shape[-1])
        as_col = lambda z: z[:DEC_BATCH].reshape(DEC_BATCH, RET_HEADS, RET_DK, 1)
        small = _pack_decode_rows(qs[:DEC_BATCH], k_row[:DEC_BATCH], v_row[:DEC_BATCH], g_da_s[:DEC_BATCH],
                                  lam_vecs, subg)
        job = _DecodeJob(page_table, ck, cv, small, lam_init, l)

        q, kp_all, kbf, vp_all, vbf, g_da, rq, rk, rv, g_ret, dec_rows = _project(
            hp, w_in_bf, l, qn_g[l][None], kn_g[l][None], cos_p, sin_p, PROMPT_TM,
            kv_stacks=(kp_all, vp_all), decode_job=job)
        a, w_in_bf = _prompt_attention(q, kbf, vbf, g_da, lam_vecs, subg, lam_init,
                                       w_next=(w_in, l + 1) if l + 1 < DEPTH else None)
        r, st = _prompt_retention(rq, rk, rv, g_ret)
        xp, hp, dec_row = _merge(a, r, w_out_bf, l, xp, g_next, MERGE_TM, decode=(job, len(dec_rows)))
        dec_rows.append(dec_row)
        sp_l.append(st)

        assert len(dec_rows) == DEC_BATCH
        a = jnp.concatenate(dec_rows, axis=0)
        rq, rk, rv, g_ret = rq_s, rk_s, rv_s, g_ret_s
        nst, r = _decode_retention(state_ret, as_col(rq), as_col(rk), as_row(rv), as_row(g_ret), l)
        a16 = _pad_rows(a.reshape(DEC_BATCH, D_DA)).astype(BF16)
        r16 = _pad_rows(r.reshape(DEC_BATCH, D_RET)).astype(BF16)
        xs, hs, _ = _merge(a16, r16, w_out_bf, l, xs, g_next, SAMPLE_ROWS)
        ks_l.append(k_row[:DEC_BATCH].reshape(DEC_BATCH, DEC_SEQ, DA_HEADS, 2, DA_HEAD_DIM))
        vs_l.append(v_row[:DEC_BATCH].reshape(DEC_BATCH, DEC_SEQ, DA_HEADS, DA_DV))
        ss_l.append(nst)

    return (xp.reshape(BATCH, SEQ, D_MODEL),
            xs[:DEC_BATCH].reshape(DEC_BATCH, DEC_SEQ, D_MODEL),
            _keys_from_cache_order(kp_all, (DEPTH, BATCH, SEQ)),
            _values_from_cache_order(vp_all, (DEPTH, BATCH, SEQ)), jnp.stack(sp_l),
            jnp.stack(ks_l), jnp.stack(vs_l), jnp.stack(ss_l))
```
